```python
import math
import jax, jax.numpy as jnp
from jax import lax
import numpy as np

D_MODEL = 2048
BATCH = 32
SEQ = 256
DEPTH = 2
DEC_BATCH = 8
DEC_SEQ = 2048
PAST_LEN = 256

GRID_W = 64
N_MIXERS = 2
N_SSD_LAYERS = (DEPTH + 1) // 2
N_ATTN_LAYERS = DEPTH // 2
D_INNER = 2 * D_MODEL
SSD_HEAD_DIM = 64
SSD_HEADS = D_INNER // SSD_HEAD_DIM
SSD_GROUPS = 8
SSD_HPG = SSD_HEADS // SSD_GROUPS
D_STATE = 128
D_CONV = 5
CHUNK = 128
CONV_DIM = D_INNER + 2 * SSD_GROUPS * D_STATE
D_IN_PROJ = D_INNER + CONV_DIM + 2 * SSD_HEADS
ATTN_HEAD_DIM = 128
N_HEADS = D_MODEL // ATTN_HEAD_DIM
N_KV_HEADS = 4
KV_REP = N_HEADS // N_KV_HEADS
QKV_DIM = (N_HEADS + 2 * N_KV_HEADS) * ATTN_HEAD_DIM
ROPE_AXIS_DIM = ATTN_HEAD_DIM // 2
ROPE_THETA = 10000.0
Q_BLOCK = 128
D_FF = ((math.ceil(8 * D_MODEL / 3) + 255) // 256) * 256
EPS = 1e-6

kernel_name = 'hybrid_ssd_gqa_diffusion_step'


def rmsnorm(x, g):
    xf = x.astype(jnp.float32)
    y = xf * lax.rsqrt(jnp.mean(xf * xf, axis=-1, keepdims=True) + EPS) * g.astype(jnp.float32)
    return y.astype(x.dtype)


def adaln(cond, w, b):
    m = jax.nn.silu(cond) @ w + b
    return jnp.split(m[:, None, :], 6, axis=-1)


def modulate(h, shift, scale):
    return h * (1.0 + scale) + shift


def swiglu(h, w_gu, w_down):
    g, u = jnp.split(h @ w_gu, 2, axis=-1)
    return (jax.nn.silu(g) * u) @ w_down


def dwconv_centred(u, w, b):
    out = lax.conv_general_dilated(u, w[:, None, :].astype(u.dtype), window_strides=(1,),
                                   padding=[(D_CONV // 2, D_CONV // 2)],
                                   dimension_numbers=('NWC', 'WIO', 'NWC'),
                                   feature_group_count=u.shape[-1])
    return out + b


def ssd_scan(x, dt, A, Bm, Cm, s0):
    b, L = x.shape[:2]
    nc = L // CHUNK

    def chunks(t):
        return t.reshape((b, nc, CHUNK) + t.shape[2:]).swapaxes(0, 1)

    lower = jnp.tril(jnp.ones((CHUNK, CHUNK), dtype=bool))

    def step(S, inp):
        xc, dtc, Bc, Cc = inp
        acs = jnp.cumsum(dtc * A, axis=1)
        xdt = xc * dtc[..., None]
        acs_t = jnp.moveaxis(acs, 1, -1)
        seg = acs_t[..., :, None] - acs_t[..., None, :]
        decay = jnp.exp(jnp.where(lower, seg, -jnp.inf))
        cb = jnp.einsum('blgn,bsgn->bgls', Cc, Bc)
        y = jnp.einsum('bgrls,bsgrp->blgrp', cb[:, :, None] * decay, xdt)
        y = y + jnp.einsum('blgn,bgrpn->blgrp', Cc, S) * jnp.exp(acs)[..., None]
        to_end = jnp.exp(acs[:, -1:] - acs)
        S = S * jnp.exp(acs[:, -1])[..., None, None] + jnp.einsum(
            'bsgn,bsgrp->bgrpn', Bc, xdt * to_end[..., None])
        return S, y

    S, ys = lax.scan(step, s0, (chunks(x), chunks(dt), chunks(Bm), chunks(Cm)))
    return ys.swapaxes(0, 1).reshape(x.shape), S


def gated_rmsnorm(y, z, g):
    yz = y * jax.nn.silu(z.astype(jnp.float32))
    yz = yz.reshape(yz.shape[:-1] + (SSD_GROUPS, D_INNER // SSD_GROUPS))
    yz = yz * lax.rsqrt(jnp.mean(yz * yz, axis=-1, keepdims=True) + EPS)
    return yz.reshape(y.shape) * g.astype(jnp.float32)


def ssd_mixer(h, s0_f, s0_b, w_in, conv_w, conv_b, a_log, dt_bias, d_skip, norm_g, w_out):
    b, L, _ = h.shape
    z, xbc, dt = jnp.split(h @ w_in, [D_INNER, D_INNER + CONV_DIM], axis=-1)
    xbc = jax.nn.silu(dwconv_centred(xbc, conv_w, conv_b)).astype(jnp.float32)
    x, Bm, Cm = jnp.split(xbc, [D_INNER, D_INNER + SSD_GROUPS * D_STATE], axis=-1)
    x = x.reshape(b, L, SSD_GROUPS, SSD_HPG, SSD_HEAD_DIM)
    Bm = Bm.reshape(b, L, SSD_GROUPS, D_STATE)
    Cm = Cm.reshape(b, L, SSD_GROUPS, D_STATE)
    dt = jax.nn.softplus(dt.astype(jnp.float32).reshape(b, L, 2, SSD_GROUPS, SSD_HPG)
                         + dt_bias.astype(jnp.float32).reshape(2, SSD_GROUPS, SSD_HPG))
    A = -jnp.exp(a_log.astype(jnp.float32)).reshape(2, SSD_GROUPS, SSD_HPG)
    dsk = d_skip.astype(jnp.float32).reshape(2, SSD_GROUPS, SSD_HPG)
    st = (b, SSD_GROUPS, SSD_HPG, SSD_HEAD_DIM, D_STATE)
    y_f, s_f = ssd_scan(x, dt[:, :, 0], A[0], Bm, Cm, s0_f.astype(jnp.float32).reshape(st))
    flip = lambda t: jnp.flip(t, axis=1)
    y_b, s_b = ssd_scan(flip(x), flip(dt[:, :, 1]), A[1], flip(Bm), flip(Cm),
                        s0_b.astype(jnp.float32).reshape(st))
    y = y_f + flip(y_b) + (dsk[0] + dsk[1])[..., None] * x
    y = gated_rmsnorm(y.reshape(b, L, D_INNER), z, norm_g).astype(h.dtype)
    hs = (b, SSD_HEADS, SSD_HEAD_DIM, D_STATE)
    return y @ w_out, s_f.reshape(hs).astype(h.dtype), s_b.reshape(hs).astype(h.dtype)


def rope_tables(L):
    rows = L // GRID_W
    row_pos = jnp.repeat(jnp.arange(rows, dtype=jnp.float32), GRID_W)
    col_pos = jnp.arange(L, dtype=jnp.float32) % GRID_W
    inv = ROPE_THETA ** (-jnp.arange(0, ROPE_AXIS_DIM, 2, dtype=jnp.float32) / ROPE_AXIS_DIM)
    ang_r = row_pos[:, None] * inv[None, :]
    ang_c = col_pos[:, None] * inv[None, :]
    return jnp.cos(ang_r), jnp.sin(ang_r), jnp.cos(ang_c), jnp.sin(ang_c)


def rotate(x, cos, sin):
    shp = (x.shape[1],) + (1,) * (x.ndim - 3) + (cos.shape[-1],)
    cos, sin = cos.reshape(shp), sin.reshape(shp)
    x1, x2 = jnp.split(x, 2, axis=-1)
    return jnp.concatenate([x1 * cos - x2 * sin, x1 * sin + x2 * cos], axis=-1)


def rope_2d(x, tabs):
    cr, sr, cc, sc = tabs
    out = jnp.concatenate([rotate(x[..., :ROPE_AXIS_DIM], cr, sr),
                           rotate(x[..., ROPE_AXIS_DIM:], cc, sc)], axis=-1)
    return out.astype(x.dtype)


def gqa_qkv(h, w_qkv, q_norm, k_norm):
    b, L, _ = h.shape
    q, k, v = jnp.split(h @ w_qkv, [N_HEADS * ATTN_HEAD_DIM, (N_HEADS + N_KV_HEADS) * ATTN_HEAD_DIM], axis=-1)
    q = rmsnorm(q.reshape(b, L, N_KV_HEADS, KV_REP, ATTN_HEAD_DIM), q_norm)
    k = rmsnorm(k.reshape(b, L, N_KV_HEADS, ATTN_HEAD_DIM), k_norm)
    v = v.reshape(b, L, N_KV_HEADS, ATTN_HEAD_DIM)
    return q, k, v


def block_attention(q, k, v):
    b, lq = q.shape[:2]
    nb = lq // Q_BLOCK
    qb = q.reshape((b, nb, Q_BLOCK) + q.shape[2:]).swapaxes(0, 1)
    scale = ATTN_HEAD_DIM ** -0.5

    def one(qblk):
        s = jnp.einsum('bqkrd,bskd->bkrqs', qblk, k).astype(jnp.float32) * scale
        p = jax.nn.softmax(s, axis=-1).astype(v.dtype)
        return jnp.einsum('bkrqs,bskd->bqkrd', p, v)

    o = lax.map(one, qb)
    return o.swapaxes(0, 1).reshape(b, lq, N_HEADS * ATTN_HEAD_DIM)


def setup_inputs(seed: int = 0) -> dict:
    key = jax.random.key(seed)
    ks = jax.random.split(key, 32)
    f32 = jnp.float32

    def nrm(k, shape, scale):
        return jax.random.normal(k, shape, f32) * scale

    dt0 = jnp.exp(jax.random.uniform(ks[13], (N_SSD_LAYERS, 2, SSD_HEADS), f32,
                                     math.log(1e-3), math.log(1e-1)))
    return {
        'x_prompt': nrm(ks[0], (BATCH, SEQ, D_MODEL), 1.0),
        'x_sample': nrm(ks[1], (DEC_BATCH, DEC_SEQ, D_MODEL), 1.0),
        'state_ssd_fwd': nrm(ks[2], (DEC_BATCH, N_SSD_LAYERS, SSD_HEADS, SSD_HEAD_DIM, D_STATE), 0.1),
        'state_ssd_bwd': nrm(ks[3], (DEC_BATCH, N_SSD_LAYERS, SSD_HEADS, SSD_HEAD_DIM, D_STATE), 0.1),
        'cache_k': nrm(ks[4], (DEC_BATCH, N_ATTN_LAYERS, PAST_LEN, N_KV_HEADS, ATTN_HEAD_DIM), 1.0),
        'cache_v': nrm(ks[5], (DEC_BATCH, N_ATTN_LAYERS, PAST_LEN, N_KV_HEADS, ATTN_HEAD_DIM), 1.0),
        'c': nrm(ks[6], (DEC_BATCH, D_MODEL), 1.0),
        'c_ctx': nrm(ks[7], (D_MODEL,), 1.0),
        'w_mod': nrm(ks[8], (DEPTH, D_MODEL, 6 * D_MODEL), 0.5 * D_MODEL ** -0.5),
        'b_mod': nrm(ks[9], (DEPTH, 6 * D_MODEL), 0.02),
        'norm_mix': 1.0 + nrm(ks[10], (DEPTH, D_MODEL), 0.02),
        'norm_ffn': 1.0 + nrm(ks[11], (DEPTH, D_MODEL), 0.02),
        'ssd_w_in': nrm(ks[12], (N_SSD_LAYERS, D_MODEL, D_IN_PROJ), D_MODEL ** -0.5),
        'ssd_conv_w': nrm(ks[14], (N_SSD_LAYERS, D_CONV, CONV_DIM), D_CONV ** -0.5),
        'ssd_conv_b': nrm(ks[15], (N_SSD_LAYERS, CONV_DIM), 0.02),
        'ssd_a_log': jnp.log(jax.random.uniform(ks[16], (N_SSD_LAYERS, 2, SSD_HEADS), f32, 1.0, 16.0)),
        'ssd_dt_bias': dt0 + jnp.log(-jnp.expm1(-dt0)),
        'ssd_d': 1.0 + nrm(ks[17], (N_SSD_LAYERS, 2, SSD_HEADS), 0.02),
        'ssd_norm': 1.0 + nrm(ks[18], (N_SSD_LAYERS, D_INNER), 0.02),
        'ssd_w_out': nrm(ks[19], (N_SSD_LAYERS, D_INNER, D_MODEL), D_INNER ** -0.5),
        'attn_w_qkv': nrm(ks[20], (N_ATTN_LAYERS, D_MODEL, QKV_DIM), D_MODEL ** -0.5),
        'attn_q_norm': 1.0 + nrm(ks[21], (N_ATTN_LAYERS, ATTN_HEAD_DIM), 0.02),
        'attn_k_norm': 1.0 + nrm(ks[22], (N_ATTN_LAYERS, ATTN_HEAD_DIM), 0.02),
        'attn_w_o': nrm(ks[23], (N_ATTN_LAYERS, N_HEADS * ATTN_HEAD_DIM, D_MODEL), (N_HEADS * ATTN_HEAD_DIM) ** -0.5),
        'ffn_w_gu': nrm(ks[24], (DEPTH, D_MODEL, 2 * D_FF), D_MODEL ** -0.5),
        'ffn_w_down': nrm(ks[25], (DEPTH, D_FF, D_MODEL), D_FF ** -0.5),
        'final_norm': 1.0 + nrm(ks[26], (D_MODEL,), 0.02),
    }


def reference(x_prompt, x_sample, state_ssd_fwd, state_ssd_bwd, cache_k, cache_v, c, c_ctx,
              w_mod, b_mod, norm_mix, norm_ffn,
              ssd_w_in, ssd_conv_w, ssd_conv_b, ssd_a_log, ssd_dt_bias, ssd_d, ssd_norm, ssd_w_out,
              attn_w_qkv, attn_q_norm, attn_k_norm, attn_w_o,
              ffn_w_gu, ffn_w_down, final_norm):
    xp, xs = x_prompt, x_sample
    bp = xp.shape[0]
    tabs = rope_tables(xs.shape[1])
    new_f, new_b, new_k, new_v = [], [], [], []
    for i in range(DEPTH):
        mp = adaln(c_ctx[None, :], w_mod[i], b_mod[i])
        ms = adaln(c, w_mod[i], b_mod[i])
        hp = modulate(rmsnorm(xp, norm_mix[i]), mp[0], mp[1])
        hs = modulate(rmsnorm(xs, norm_mix[i]), ms[0], ms[1])
        j = i // N_MIXERS
        if i % N_MIXERS == 0:
            prm = (ssd_w_in[j], ssd_conv_w[j], ssd_conv_b[j], ssd_a_log[j], ssd_dt_bias[j],
                   ssd_d[j], ssd_norm[j], ssd_w_out[j])
            zeros = jnp.zeros((bp, SSD_HEADS, SSD_HEAD_DIM, D_STATE), xp.dtype)
            op, sf, sb = ssd_mixer(hp, zeros, zeros, *prm)
            os_, _, _ = ssd_mixer(hs, state_ssd_fwd[:, j], state_ssd_bwd[:, j], *prm)
            new_f.append(sf)
            new_b.append(sb)
        else:
            qp, kp, vp = gqa_qkv(hp, attn_w_qkv[j], attn_q_norm[j], attn_k_norm[j])
            op = block_attention(qp, kp, vp) @ attn_w_o[j]
            qs, ks_, vs = gqa_qkv(hs, attn_w_qkv[j], attn_q_norm[j], attn_k_norm[j])
            qs = rope_2d(qs, tabs)
            ks_ = rope_2d(ks_, tabs)
            k_all = jnp.concatenate([cache_k[:, j].astype(ks_.dtype), ks_], axis=1)
            v_all = jnp.concatenate([cache_v[:, j].astype(vs.dtype), vs], axis=1)
            os_ = block_attention(qs, k_all, v_all) @ attn_w_o[j]
            new_k.append(kp)
            new_v.append(vp)
        xp = xp + mp[2] * op
        xs = xs + ms[2] * os_
        hp = modulate(rmsnorm(xp, norm_ffn[i]), mp[3], mp[4])
        hs = modulate(rmsnorm(xs, norm_ffn[i]), ms[3], ms[4])
        xp = xp + mp[5] * swiglu(hp, ffn_w_gu[i], ffn_w_down[i])
        xs = xs + ms[5] * swiglu(hs, ffn_w_gu[i], ffn_w_down[i])
    y_prompt = rmsnorm(xp, final_norm)
    y_sample = rmsnorm(xs, final_norm)
    return (y_prompt, y_sample, jnp.stack(new_f, axis=1), jnp.stack(new_b, axis=1),
            jnp.stack(new_k, axis=1), jnp.stack(new_v, axis=1))
```

```python
import functools
import math

import jax
import jax.numpy as jnp
from jax import lax
from jax.experimental import pallas as pl
from jax.experimental.pallas import tpu as pltpu

F32 = jnp.float32
BF16 = jnp.bfloat16

EPS = 1e-6
GRID_W = 64
SSD_HEAD_DIM = 64
SSD_GROUPS = 8
D_STATE = 128
D_CONV = 5
CHUNK = 128
ATTN_HEAD_DIM = 128
N_KV_HEADS = 4
ROPE_THETA = 10000.0

VMEM_LIMIT_BYTES = 56 * 1024 * 1024
LANES = 128
MAX_TOKEN_TILE = 1024
SCAN_ROWS = 2 * CHUNK
CONV_ROWS = 256
CONV_HALO = 16


def _params(n_axes):
    return pltpu.CompilerParams(dimension_semantics=("arbitrary",) * n_axes,
                                vmem_limit_bytes=VMEM_LIMIT_BYTES)


def _silu(v):
    return v * jax.nn.sigmoid(v)


def _softplus(v):
    return jnp.maximum(v, 0.0) + jnp.log1p(jnp.exp(-jnp.abs(v)))


def _dot(a, b):
    return jnp.dot(a, b, preferred_element_type=F32)


def _split2(v):
    hi = v.astype(BF16)
    lo = (v - hi.astype(F32)).astype(BF16)
    return hi, lo


def _split3(v):
    hi = v.astype(BF16)
    r = v - hi.astype(F32)
    mid = r.astype(BF16)
    lo = (r - mid.astype(F32)).astype(BF16)
    return hi, mid, lo


def _adaln_kernel(c_ref, w_ref, b_ref, o_ref):
    s = _silu(c_ref[...]).astype(BF16)
    o_ref[0] = _dot(s, w_ref[0].astype(BF16)) + b_ref[0]


def _adaln(cond, w_mod, b_mod):
    depth, d, n = w_mod.shape
    rows = cond.shape[0]
    tn = 1536
    assert n % tn == 0
    return pl.pallas_call(
        _adaln_kernel,
        grid=(depth, n // tn),
        in_specs=[pl.BlockSpec((rows, d), lambda l, j: (0, 0)),
                  pl.BlockSpec((1, d, tn), lambda l, j: (l, 0, j)),
                  pl.BlockSpec((1, 1, tn), lambda l, j: (l, 0, j))],
        out_specs=pl.BlockSpec((1, rows, tn), lambda l, j: (l, 0, j)),
        out_shape=jax.ShapeDtypeStruct((depth, rows, n), F32),
        compiler_params=_params(2),
        name="adaln",
    )(cond, w_mod, b_mod.reshape(depth, 1, n))


class _Geom:
    def __init__(self, n_p, len_p, n_s, len_s):
        self.n_p, self.len_p, self.n_s, self.len_s = n_p, len_p, n_s, len_s
        self.t_p = n_p * len_p
        self.t_s = n_s * len_s
        self.t = self.t_p + self.t_s
        self.tm = math.gcd(math.gcd(self.t_p, len_s), MAX_TOKEN_TILE)

    def mod_row(self, i, tile):
        npt = self.t_p // tile
        per_seq = self.len_s // tile
        return jnp.where(i < npt, 0, 1 + (i - npt) // per_seq)


def _nm_kernel(x_ref, mod_ref, g_ref, *rest, n_w, shift_row, act, row_chunk):
    w_refs = rest[:n_w]
    o_ref = rest[n_w]
    h_ref = rest[n_w + 1]

    @pl.when(pl.program_id(1) == 0)
    def _():
        shift = mod_ref[0, shift_row:shift_row + 1, :]
        scale1 = 1.0 + mod_ref[0, shift_row + 1:shift_row + 2, :]
        g = g_ref[...]

        def body(r, carry):
            rows = pl.ds(pl.multiple_of(r * row_chunk, row_chunk), row_chunk)
            x = x_ref[rows, :]
            ms = jnp.mean(x * x, axis=-1, keepdims=True)
            y = x * lax.rsqrt(ms + EPS) * g
            h_ref[rows, :] = (y * scale1 + shift).astype(BF16)
            return carry

        lax.fori_loop(0, x_ref.shape[0] // row_chunk, body, 0)

    h = h_ref[...]
    if act == "swiglu":
        gate = _dot(h, w_refs[0][...])
        up = _dot(h, w_refs[1][...])
        o_ref[...] = (_silu(gate) * up).astype(o_ref.dtype)
    else:
        o_ref[...] = _dot(h, w_refs[0][...]).astype(o_ref.dtype)


def _norm_mod_matmul(geom, x, mods, gain, w, col_offsets, n_cols, tn, shift_row, act, out_dtype, name):
    t, d = x.shape
    tm = geom.tm
    assert n_cols % tn == 0 and all(off % tn == 0 for off in col_offsets)
    w_specs = [pl.BlockSpec((d, tn), functools.partial(lambda i, j, ob: (0, ob + j), ob=off // tn))
               for off in col_offsets]
    return pl.pallas_call(
        functools.partial(_nm_kernel, n_w=len(col_offsets), shift_row=shift_row, act=act,
                          row_chunk=min(tm, 256)),
        grid=(t // tm, n_cols // tn),
        in_specs=[pl.BlockSpec((tm, d), lambda i, j: (i, 0)),
                  pl.BlockSpec((1, 6, d), lambda i, j: (geom.mod_row(i, tm), 0, 0)),
                  pl.BlockSpec((1, d), lambda i, j: (0, 0))] + w_specs,
        out_specs=pl.BlockSpec((tm, tn), lambda i, j: (i, j)),
        out_shape=jax.ShapeDtypeStruct((t, n_cols), out_dtype),
        scratch_shapes=[pltpu.VMEM((tm, d), BF16)],
        compiler_params=_params(2),
        name=name,
    )(x, mods, gain.reshape(1, d), *([w] * len(col_offsets)))


def _gate_res_kernel(a_ref, w_ref, x_ref, mod_ref, o_ref, *, gate_row):
    acc = _dot(a_ref[...], w_ref[...])
    o_ref[...] = x_ref[...] + mod_ref[0, gate_row:gate_row + 1, :] * acc


def _matmul_gate_res(geom, a, w, x, mods, gate_row, tn, name):
    t, k = a.shape
    d = w.shape[1]
    tm = geom.tm
    assert d % tn == 0
    return pl.pallas_call(
        functools.partial(_gate_res_kernel, gate_row=gate_row),
        grid=(t // tm, d // tn),
        in_specs=[pl.BlockSpec((tm, k), lambda i, j: (i, 0)),
                  pl.BlockSpec((k, tn), lambda i, j: (0, j)),
                  pl.BlockSpec((tm, tn), lambda i, j: (i, j)),
                  pl.BlockSpec((1, 6, tn), lambda i, j: (geom.mod_row(i, tm), 0, j))],
        out_specs=pl.BlockSpec((tm, tn), lambda i, j: (i, j)),
        out_shape=jax.ShapeDtypeStruct((t, d), F32),
        compiler_params=_params(2),
        name=name,
    )(a, w, x, mods)


def _rmsnorm_kernel(x_ref, g_ref, o_ref):
    x = x_ref[...]
    ms = jnp.mean(x * x, axis=-1, keepdims=True)
    o_ref[...] = x * lax.rsqrt(ms + EPS) * g_ref[...]


def _final_norm(x, gain, row0, n_rows, name):
    d = x.shape[1]
    tr = 256
    assert row0 % tr == 0 and n_rows % tr == 0
    return pl.pallas_call(
        _rmsnorm_kernel,
        grid=(n_rows // tr,),
        in_specs=[pl.BlockSpec((tr, d), lambda i: (row0 // tr + i, 0)),
                  pl.BlockSpec((1, d), lambda i: (0, 0))],
        out_specs=pl.BlockSpec((tr, d), lambda i: (i, 0)),
        out_shape=jax.ShapeDtypeStruct((n_rows, d), F32),
        compiler_params=_params(1),
        name=name,
    )(x, gain.reshape(1, d))


def _conv_kernel(u_ref, prev_ref, next_ref, w_ref, b_ref, o_ref, ext_ref, *, n_p_blocks, p_blocks, s_blocks):
    i = pl.program_id(0)
    lb = u_ref.shape[0]
    is_p = i < n_p_blocks
    pos = jnp.where(is_p, i % p_blocks, (i - n_p_blocks) % s_blocks)
    last = jnp.where(is_p, p_blocks, s_blocks) - 1
    ext_ref[0:CONV_HALO, :] = jnp.where(pos > 0, prev_ref[...].astype(F32), 0.0)
    ext_ref[CONV_HALO:CONV_HALO + lb, :] = u_ref[...].astype(F32)
    ext_ref[CONV_HALO + lb:2 * CONV_HALO + lb, :] = jnp.where(pos < last, next_ref[...].astype(F32), 0.0)
    acc = jnp.broadcast_to(b_ref[...], o_ref.shape)
    for k in range(D_CONV):
        r0 = CONV_HALO - D_CONV // 2 + k
        acc = acc + w_ref[k:k + 1, :] * ext_ref[r0:r0 + lb, :]
    o_ref[...] = _silu(acc).astype(o_ref.dtype)


def _ssd_conv(geom, xbc, conv_w, conv_b):
    t, c = xbc.shape
    lb, ct = CONV_ROWS, 512
    assert geom.len_p % lb == 0 and geom.len_s % lb == 0 and c % ct == 0
    hb = lb // CONV_HALO
    n_halo_blocks = t // CONV_HALO
    kern = functools.partial(_conv_kernel, n_p_blocks=geom.t_p // lb, p_blocks=geom.len_p // lb,
                             s_blocks=geom.len_s // lb)
    return pl.pallas_call(
        kern,
        grid=(t // lb, c // ct),
        in_specs=[pl.BlockSpec((lb, ct), lambda i, j: (i, j)),
                  pl.BlockSpec((CONV_HALO, ct), lambda i, j: (jnp.maximum(i * hb - 1, 0), j)),
                  pl.BlockSpec((CONV_HALO, ct), lambda i, j: (jnp.minimum((i + 1) * hb, n_halo_blocks - 1), j)),
                  pl.BlockSpec((D_CONV, ct), lambda i, j: (0, j)),
                  pl.BlockSpec((1, ct), lambda i, j: (0, j))],
        out_specs=pl.BlockSpec((lb, ct), lambda i, j: (i, j)),
        out_shape=jax.ShapeDtypeStruct((t, c), BF16),
        scratch_shapes=[pltpu.VMEM((lb + 2 * CONV_HALO, ct), F32)],
        compiler_params=_params(2),
        name="ssd_conv",
    )(xbc, xbc, xbc, conv_w, conv_b.reshape(1, c))


def _scan_kernel(*refs, rev, fuse_norm, n_steps, n_p_steps, p_steps, s_steps, hpg):
    if fuse_norm:
        (x_ref, b_ref, c_ref, dtc_ref, dtr_ref, pc_ref, pr_ref, s0_ref, yo_ref, z_ref, dsk_ref, ng_ref,
         y_ref, sfin_ref, st_ref) = refs
    else:
        (x_ref, b_ref, c_ref, dtc_ref, dtr_ref, pc_ref, pr_ref, s0_ref,
         y_ref, sfin_ref, st_ref) = refs
    q = CHUNK
    hd = SSD_HEAD_DIM
    width = hpg * hd
    step = pl.program_id(1)
    blk = (n_steps - 1 - step) if rev else step
    is_p = blk < n_p_steps
    pos = jnp.where(is_p, blk % p_steps, (blk - n_p_steps) % s_steps)
    last = jnp.where(is_p, p_steps, s_steps) - 1
    seq_start = (pos == last) if rev else (pos == 0)
    seq_end = (pos == 0) if rev else (pos == last)

    @pl.when(jnp.logical_and(seq_start, is_p))
    def _():
        st_ref[...] = jnp.zeros_like(st_ref)

    @pl.when(jnp.logical_and(seq_start, jnp.logical_not(is_p)))
    def _():
        st_ref[...] = s0_ref[...].T

    a_col = -jnp.exp(pc_ref[0:1, :])
    bias_col = pc_ref[1:2, :]
    a_row = -jnp.exp(pr_ref[:, 0:1])
    bias_row = pr_ref[:, 1:2]

    li = lax.broadcasted_iota(jnp.int32, (q, q), 0)
    si = lax.broadcasted_iota(jnp.int32, (q, q), 1)
    lower = li >= si
    upper = li <= si
    keep = upper if rev else lower
    tri_col = keep.astype(BF16)
    tri_row = (lower if rev else upper).astype(BF16)
    expand = (lax.broadcasted_iota(jnp.int32, (hpg, width), 1) // hd
              == lax.broadcasted_iota(jnp.int32, (hpg, width), 0)).astype(BF16)
    lane = lax.broadcasted_iota(jnp.int32, (q, 2 * hd), 1)

    def widen(v):
        hi, lo = _split2(v)
        return _dot(hi, expand) + _dot(lo, expand)

    n_chunks = x_ref.shape[0] // q
    order = range(n_chunks - 1, -1, -1) if rev else range(n_chunks)
    for ci in order:
        r0 = ci * q
        x = x_ref[r0:r0 + q, :]
        bm = b_ref[r0:r0 + q, :]
        cm = c_ref[r0:r0 + q, :]
        dt_c = _softplus(dtc_ref[r0:r0 + q, :] + bias_col)
        dt_r = _softplus(dtr_ref[:, r0:r0 + q] + bias_row)
        cs_c = sum(_dot(tri_col, part) for part in _split3(dt_c * a_col))
        cs_r = sum(_dot(part, tri_row) for part in _split3(dt_r * a_row))
        edge = 0 if rev else q - 1
        total = cs_c[edge:edge + 1, :]
        e_x = widen(jnp.exp(cs_c))
        w_x = widen(dt_c * jnp.exp(total - cs_c))

        cb = lax.dot_general(cm, bm, (((1,), (1,)), ((), ())), preferred_element_type=F32)
        state = st_ref[...]
        y = _dot(cm, state.astype(BF16)) * e_x
        parts = []
        for p in range(hpg // 2):
            ms = []
            for r in (2 * p, 2 * p + 1):
                seg = cs_c[:, r:r + 1] - cs_r[r:r + 1, :]
                dec = jnp.exp(jnp.where(keep, seg, -jnp.inf))
                ms.append((cb * dec * dt_r[r:r + 1, :]).astype(BF16))
            x2 = x[:, 2 * p * hd:2 * (p + 1) * hd]
            zero = jnp.zeros_like(x2)
            rhs = jnp.concatenate([jnp.where(lane < hd, x2, zero), jnp.where(lane >= hd, x2, zero)], axis=0)
            parts.append(_dot(jnp.concatenate(ms, axis=1), rhs))
        y = y + jnp.concatenate(parts, axis=1)

        xw = (x.astype(F32) * w_x).astype(BF16)
        st_ref[...] = state * e_x[edge:edge + 1, :] + lax.dot_general(
            bm, xw, (((0,), (0,)), ((), ())), preferred_element_type=F32)

        if fuse_norm:
            y = y + yo_ref[r0:r0 + q, :].astype(F32) + dsk_ref[...] * x.astype(F32)
            yz = y * _silu(z_ref[r0:r0 + q, :].astype(F32))
            yz = yz * lax.rsqrt(jnp.mean(yz * yz, axis=-1, keepdims=True) + EPS)
            y = yz * ng_ref[...]
        y_ref[r0:r0 + q, :] = y.astype(y_ref.dtype)

    @pl.when(jnp.logical_and(seq_end, is_p))
    def _():
        sfin_ref[...] = st_ref[...].T


def _ssd_scan(geom, direction, xbc, dt_col, dt_row, p_col, p_row, s0, y_other=None, z=None, dsk=None, ng=None):
    t = xbc.shape[0]
    g_n = SSD_GROUPS
    hpg = dt_col.shape[-1]
    width = hpg * SSD_HEAD_DIM
    n = D_STATE
    rb = SCAN_ROWS
    assert geom.len_p % rb == 0 and geom.len_s % rb == 0
    n_steps = t // rb
    n_p_steps, p_steps, s_steps = geom.t_p // rb, geom.len_p // rb, geom.len_s // rb
    rev = direction == 1
    fuse = y_other is not None
    x_cols = g_n * width
    b_blk0 = x_cols // n
    c_blk0 = (x_cols + g_n * n) // n

    def blk(s):
        return (n_steps - 1 - s) if rev else s

    def s0_idx(g, s):
        return (jnp.clip((blk(s) - n_p_steps) // s_steps, 0, geom.n_s - 1), g, 0, 0)

    def sfin_idx(g, s):
        return (jnp.minimum(blk(s) // p_steps, geom.n_p - 1), g, 0, 0)

    in_specs = [pl.BlockSpec((rb, width), lambda g, s: (blk(s), g)),
                pl.BlockSpec((rb, n), lambda g, s: (blk(s), b_blk0 + g)),
                pl.BlockSpec((rb, n), lambda g, s: (blk(s), c_blk0 + g)),
                pl.BlockSpec((None, None, rb, hpg), lambda g, s: (direction, g, blk(s), 0)),
                pl.BlockSpec((None, None, hpg, rb), lambda g, s: (direction, g, 0, blk(s))),
                pl.BlockSpec((None, None, 2, hpg), lambda g, s: (direction, g, 0, 0)),
                pl.BlockSpec((None, None, hpg, 2), lambda g, s: (direction, g, 0, 0)),
                pl.BlockSpec((None, None, width, n), s0_idx)]
    args = [xbc, xbc, xbc, dt_col, dt_row, p_col, p_row, s0]
    if fuse:
        in_specs += [pl.BlockSpec((rb, width), lambda g, s: (blk(s), g)),
                     pl.BlockSpec((rb, width), lambda g, s: (blk(s), g)),
                     pl.BlockSpec((None, 1, width), lambda g, s: (g, 0, 0)),
                     pl.BlockSpec((None, 1, width), lambda g, s: (g, 0, 0))]
        args += [y_other, z, dsk, ng]
    kern = functools.partial(_scan_kernel, rev=rev, fuse_norm=fuse, n_steps=n_steps, n_p_steps=n_p_steps,
                             p_steps=p_steps, s_steps=s_steps, hpg=hpg)
    return pl.pallas_call(
        kern,
        grid=(g_n, n_steps),
        in_specs=in_specs,
        out_specs=[pl.BlockSpec((rb, width), lambda g, s: (blk(s), g)),
                   pl.BlockSpec((None, None, width, n), sfin_idx)],
        out_shape=[jax.ShapeDtypeStruct((t, x_cols), BF16),
                   jax.ShapeDtypeStruct((geom.n_p, g_n, width, n), F32)],
        scratch_shapes=[pltpu.VMEM((n, width), F32)],
        compiler_params=_params(2),
        name="ssd_scan_bwd" if rev else "ssd_scan_fwd",
    )(*args)


def _qk_prep_kernel(qkv_ref, cos_ref, sin_ref, qg_ref, kg_ref, q_ref, k_ref, v_ref, kf_ref, vf_ref,
                    *, n_p_tiles, n_heads, n_kv):
    i = pl.program_id(0)
    hd = ATTN_HEAD_DIM
    is_p = i < n_p_tiles
    cos = jnp.where(is_p, 1.0, cos_ref[...])
    sin = jnp.where(is_p, 0.0, sin_ref[...])
    lane = lax.broadcasted_iota(jnp.int32, cos.shape, 1)
    first_half = (lane % (hd // 2)) < (hd // 4)

    def norm_rope(v, gain):
        n = v * lax.rsqrt(jnp.mean(v * v, axis=-1, keepdims=True) + EPS) * gain
        partner = jnp.where(first_half, pltpu.roll(n, hd - hd // 4, 1), pltpu.roll(n, hd // 4, 1))
        return n, n * cos + partner * sin

    for h in range(n_heads):
        _, roped = norm_rope(qkv_ref[:, h * hd:(h + 1) * hd], qg_ref[...])
        q_ref[:, h * hd:(h + 1) * hd] = roped.astype(q_ref.dtype)
    k0 = n_heads * hd
    v0 = (n_heads + n_kv) * hd
    for h in range(n_kv):
        normed, roped = norm_rope(qkv_ref[:, k0 + h * hd:k0 + (h + 1) * hd], kg_ref[...])
        k_ref[:, h * hd:(h + 1) * hd] = roped.astype(k_ref.dtype)

        @pl.when(is_p)
        def _():
            kf_ref[:, h * hd:(h + 1) * hd] = normed
    v = qkv_ref[:, v0:v0 + n_kv * hd]
    v_ref[...] = v.astype(v_ref.dtype)

    @pl.when(is_p)
    def _():
        vf_ref[...] = v


def _qk_prep(geom, qkv, cos, sin, q_gain, k_gain, n_heads):
    t = qkv.shape[0]
    hd, n_kv = ATTN_HEAD_DIM, N_KV_HEADS
    tr = 256
    assert geom.t_p % tr == 0 and geom.len_s % tr == 0
    n_p_tiles = geom.t_p // tr
    per_seq = geom.len_s // tr

    def tab_idx(i):
        return (jnp.where(i < n_p_tiles, 0, (i - n_p_tiles) % per_seq), 0)

    def ctx_idx(i):
        return (jnp.minimum(i, n_p_tiles - 1), 0)

    kern = functools.partial(_qk_prep_kernel, n_p_tiles=n_p_tiles, n_heads=n_heads, n_kv=n_kv)
    return pl.pallas_call(
        kern,
        grid=(t // tr,),
        in_specs=[pl.BlockSpec((tr, qkv.shape[1]), lambda i: (i, 0)),
                  pl.BlockSpec((tr, hd), tab_idx),
                  pl.BlockSpec((tr, hd), tab_idx),
                  pl.BlockSpec((1, hd), lambda i: (0, 0)),
                  pl.BlockSpec((1, hd), lambda i: (0, 0))],
        out_specs=[pl.BlockSpec((tr, n_heads * hd), lambda i: (i, 0)),
                   pl.BlockSpec((tr, n_kv * hd), lambda i: (i, 0)),
                   pl.BlockSpec((tr, n_kv * hd), lambda i: (i, 0)),
                   pl.BlockSpec((tr, n_kv * hd), ctx_idx),
                   pl.BlockSpec((tr, n_kv * hd), ctx_idx)],
        out_shape=[jax.ShapeDtypeStruct((t, n_heads * hd), BF16),
                   jax.ShapeDtypeStruct((t, n_kv * hd), BF16),
                   jax.ShapeDtypeStruct((t, n_kv * hd), BF16),
                   jax.ShapeDtypeStruct((geom.t_p, n_kv * hd), F32),
                   jax.ShapeDtypeStruct((geom.t_p, n_kv * hd), F32)],
        compiler_params=_params(1),
        name="qk_prep",
    )(qkv, cos, sin, q_gain.reshape(1, hd), k_gain.reshape(1, hd))


def _attn_kernel(*refs, rep, aliased):
    q_ref, k_ref, v_ref = refs[0], refs[1], refs[2]
    o_ref = refs[4] if aliased else refs[3]
    hd = ATTN_HEAD_DIM
    rows = q_ref.shape[0]
    k = k_ref[...]
    v = v_ref[...]
    q = jnp.concatenate([q_ref[:, r * hd:(r + 1) * hd] for r in range(rep)], axis=0)
    s = lax.dot_general(q, k, (((1,), (1,)), ((), ())), preferred_element_type=F32) * (hd ** -0.5)
    p = jnp.exp(s - jnp.max(s, axis=-1, keepdims=True))
    denom = jnp.sum(p, axis=-1, keepdims=True)
    o = _dot(p.astype(BF16), v) / denom
    for r in range(rep):
        o_ref[:, r * hd:(r + 1) * hd] = o[r * rows:(r + 1) * rows, :].astype(o_ref.dtype)


def _attention(q, k_ctx, v_ctx, k_lat, v_lat, geom, rep):
    t, qd = q.shape
    hd, n_kv = ATTN_HEAD_DIM, N_KV_HEADS
    qb = 128
    out_shape = jax.ShapeDtypeStruct((t, qd), BF16)
    blk_q = (qb, rep * hd)
    p_blocks = geom.len_p // qb
    o = pl.pallas_call(
        functools.partial(_attn_kernel, rep=rep, aliased=False),
        grid=(geom.n_p, n_kv, p_blocks),
        in_specs=[pl.BlockSpec(blk_q, lambda b, h, i: (b * p_blocks + i, h)),
                  pl.BlockSpec((geom.len_p, hd), lambda b, h, i: (b, h)),
                  pl.BlockSpec((geom.len_p, hd), lambda b, h, i: (b, h))],
        out_specs=pl.BlockSpec(blk_q, lambda b, h, i: (b * p_blocks + i, h)),
        out_shape=out_shape,
        compiler_params=_params(3),
        name="attn_ctx",
    )(q, k_ctx, v_ctx)
    s_blocks = geom.len_s // qb
    q0 = geom.t_p // qb
    l_all = k_lat.shape[1]
    return pl.pallas_call(
        functools.partial(_attn_kernel, rep=rep, aliased=True),
        grid=(geom.n_s, n_kv, s_blocks),
        in_specs=[pl.BlockSpec(blk_q, lambda b, h, i: (q0 + b * s_blocks + i, h)),
                  pl.BlockSpec((None, l_all, hd), lambda b, h, i: (b, 0, h)),
                  pl.BlockSpec((None, l_all, hd), lambda b, h, i: (b, 0, h)),
                  pl.BlockSpec(memory_space=pl.ANY)],
        out_specs=pl.BlockSpec(blk_q, lambda b, h, i: (q0 + b * s_blocks + i, h)),
        out_shape=out_shape,
        input_output_aliases={3: 0},
        compiler_params=_params(3),
        name="attn_lat",
    )(q, k_lat, v_lat, o)


def _rope_tables(length):
    half = ATTN_HEAD_DIM // 2
    pos = jnp.arange(length, dtype=jnp.int32)
    row_pos = (pos // GRID_W).astype(F32)
    col_pos = (pos % GRID_W).astype(F32)
    inv = ROPE_THETA ** (-jnp.arange(0, half, 2, dtype=F32) / half)
    ang_r = row_pos[:, None] * inv[None, :]
    ang_c = col_pos[:, None] * inv[None, :]
    cos = jnp.concatenate([jnp.cos(ang_r)] * 2 + [jnp.cos(ang_c)] * 2, axis=-1)
    sin = jnp.concatenate([-jnp.sin(ang_r), jnp.sin(ang_r), -jnp.sin(ang_c), jnp.sin(ang_c)], axis=-1)
    return cos, sin


def kernel(x_prompt, x_sample, state_ssd_fwd, state_ssd_bwd, cache_k, cache_v, c, c_ctx, w_mod, b_mod, norm_mix, norm_ffn, ssd_w_in, ssd_conv_w, ssd_conv_b, ssd_a_log, ssd_dt_bias, ssd_d, ssd_norm, ssd_w_out, attn_w_qkv, attn_q_norm, attn_k_norm, attn_w_o, ffn_w_gu, ffn_w_down, final_norm):
    n_p, len_p, d = x_prompt.shape
    n_s, len_s, _ = x_sample.shape
    depth = w_mod.shape[0]
    geom = _Geom(n_p, len_p, n_s, len_s)
    d_ff = ffn_w_down.shape[1]
    d_inner = ssd_w_out.shape[1]
    heads = d_inner // SSD_HEAD_DIM
    hpg = heads // SSD_GROUPS
    gw = hpg * SSD_HEAD_DIM
    conv_dim = d_inner + 2 * SSD_GROUPS * D_STATE
    n_heads = attn_w_o.shape[1] // ATTN_HEAD_DIM
    rep = n_heads // N_KV_HEADS

    x = jnp.concatenate([x_prompt.reshape(geom.t_p, d), x_sample.reshape(geom.t_s, d)], axis=0)

    cond_rows = -(-(1 + n_s) // 8) * 8
    cond = jnp.zeros((cond_rows, d), F32).at[0].set(c_ctx).at[1:1 + n_s].set(c)
    mods_all = _adaln(cond, w_mod, b_mod).reshape(depth, cond_rows, 6, d)

    new_f, new_b, new_k, new_v = [], [], [], []
    for i in range(depth):
        mods = mods_all[i]
        j = i // 2
        if i % 2 == 0:
            w_in = ssd_w_in[j].astype(BF16)
            z = _norm_mod_matmul(geom, x, mods, norm_mix[i], w_in, [0], d_inner, 512, 0, None, BF16, "ssd_in_z")
            xbc = _norm_mod_matmul(geom, x, mods, norm_mix[i], w_in, [d_inner], conv_dim, 512, 0, None, BF16,
                                   "ssd_in_xbc")
            dt_raw = _norm_mod_matmul(geom, x, mods, norm_mix[i], w_in, [d_inner + conv_dim], 2 * heads,
                                      2 * heads, 0, None, F32, "ssd_in_dt")
            xbc = _ssd_conv(geom, xbc, ssd_conv_w[j], ssd_conv_b[j])
            dt4 = dt_raw.reshape(geom.t, 2, SSD_GROUPS, hpg)
            dt_col = dt4.transpose(1, 2, 0, 3)
            dt_row = dt4.transpose(1, 2, 3, 0)
            prm = jnp.stack([ssd_a_log[j], ssd_dt_bias[j]], axis=1).reshape(2, 2, SSD_GROUPS, hpg)
            p_col = prm.transpose(0, 2, 1, 3)
            p_row = prm.transpose(0, 2, 3, 1)
            dsk = jnp.repeat(ssd_d[j, 0] + ssd_d[j, 1], SSD_HEAD_DIM).reshape(SSD_GROUPS, 1, gw)
            ng = ssd_norm[j].reshape(SSD_GROUPS, 1, gw)
            s0_f = state_ssd_fwd[:, j].reshape(n_s, SSD_GROUPS, gw, D_STATE)
            s0_b = state_ssd_bwd[:, j].reshape(n_s, SSD_GROUPS, gw, D_STATE)
            y_b, s_b = _ssd_scan(geom, 1, xbc, dt_col, dt_row, p_col, p_row, s0_b)
            y, s_f = _ssd_scan(geom, 0, xbc, dt_col, dt_row, p_col, p_row, s0_f, y_b, z, dsk, ng)
            new_f.append(s_f.reshape(n_p, heads, SSD_HEAD_DIM, D_STATE))
            new_b.append(s_b.reshape(n_p, heads, SSD_HEAD_DIM, D_STATE))
            x = _matmul_gate_res(geom, y, ssd_w_out[j].astype(BF16), x, mods, 2, 512, "ssd_out")
        else:
            w_qkv = attn_w_qkv[j].astype(BF16)
            qkv = _norm_mod_matmul(geom, x, mods, norm_mix[i], w_qkv, [0], w_qkv.shape[1], 512, 0, None, F32,
                                   "attn_qkv")
            cos, sin = _rope_tables(len_s)
            q, k, v, k_ctx, v_ctx = _qk_prep(geom, qkv, cos, sin, attn_q_norm[j], attn_k_norm[j], n_heads)
            kv_w = N_KV_HEADS * ATTN_HEAD_DIM
            k_lat = jnp.concatenate([cache_k[:, j].reshape(n_s, -1, kv_w).astype(BF16),
                                     k[geom.t_p:].reshape(n_s, len_s, kv_w)], axis=1)
            v_lat = jnp.concatenate([cache_v[:, j].reshape(n_s, -1, kv_w).astype(BF16),
                                     v[geom.t_p:].reshape(n_s, len_s, kv_w)], axis=1)
            o = _attention(q, k, v, k_lat, v_lat, geom, rep)
            new_k.append(k_ctx.reshape(n_p, len_p, N_KV_HEADS, ATTN_HEAD_DIM))
            new_v.append(v_ctx.reshape(n_p, len_p, N_KV_HEADS, ATTN_HEAD_DIM))
            x = _matmul_gate_res(geom, o, attn_w_o[j].astype(BF16), x, mods, 2, 512, "attn_out")
        w_gu = ffn_w_gu[i].astype(BF16)
        hidden = _norm_mod_matmul(geom, x, mods, norm_ffn[i], w_gu, [0, d_ff], d_ff, 512, 3, "swiglu", BF16,
                                  "ffn_up")
        x = _matmul_gate_res(geom, hidden, ffn_w_down[i].astype(BF16), x, mods, 5, 512, "ffn_down")

    y_prompt = _final_norm(x, final_norm, 0, geom.t_p, "final_norm_ctx").reshape(n_p, len_p, d)
    y_sample = _final_norm(x, final_norm, geom.t_p, geom.t_s, "final_norm_lat").reshape(n_s, len_s, d)
    return (y_prompt, y_sample, jnp.stack(new_f, axis=1), jnp.stack(new_b, axis=1),
            jnp.stack(new_k, axis=1), jnp.stack(new_v, axis=1))
```

```python
import functools
import math

import jax
import jax.numpy as jnp
from jax import lax
from jax.experimental import pallas as pl
from jax.experimental.pallas import tpu as pltpu

F32 = jnp.float32
BF16 = jnp.bfloat16

EPS = 1e-6
GRID_W = 64
SSD_HEAD_DIM = 64
SSD_GROUPS = 8
D_STATE = 128
D_CONV = 5
CHUNK = 128
ATTN_HEAD_DIM = 128
N_KV_HEADS = 4
ROPE_THETA = 10000.0

VMEM_LIMIT_BYTES = 56 * 1024 * 1024
LANES = 128
MAX_TOKEN_TILE = 1024
SCAN_ROWS = 2 * CHUNK
CONV_ROWS = 256
CONV_HALO = 16
ATTN_LAT_HEADS_PER_CHAIN = 2


def _params(n_axes):
    return pltpu.CompilerParams(dimension_semantics=("arbitrary",) * n_axes,
                                vmem_limit_bytes=VMEM_LIMIT_BYTES)


def _silu(v):
    return v * jax.nn.sigmoid(v)


def _softplus(v):
    return jnp.maximum(v, 0.0) + jnp.log1p(jnp.exp(-jnp.abs(v)))


def _dot(a, b):
    return jnp.dot(a, b, preferred_element_type=F32)


def _split2(v):
    hi = v.astype(BF16)
    lo = (v - hi.astype(F32)).astype(BF16)
    return hi, lo


def _split3(v):
    hi = v.astype(BF16)
    r = v - hi.astype(F32)
    mid = r.astype(BF16)
    lo = (r - mid.astype(F32)).astype(BF16)
    return hi, mid, lo


def _adaln_kernel(c_ref, w_ref, b_ref, o_ref):
    s = _silu(c_ref[...]).astype(BF16)
    o_ref[0] = _dot(s, w_ref[0].astype(BF16)) + b_ref[0]


def _adaln(cond, w_mod, b_mod):
    depth, d, n = w_mod.shape
    rows = cond.shape[0]
    tn = 1536
    assert n % tn == 0
    return pl.pallas_call(
        _adaln_kernel,
        grid=(depth, n // tn),
        in_specs=[pl.BlockSpec((rows, d), lambda l, j: (0, 0)),
                  pl.BlockSpec((1, d, tn), lambda l, j: (l, 0, j)),
                  pl.BlockSpec((1, 1, tn), lambda l, j: (l, 0, j))],
        out_specs=pl.BlockSpec((1, rows, tn), lambda l, j: (l, 0, j)),
        out_shape=jax.ShapeDtypeStruct((depth, rows, n), F32),
        compiler_params=_params(2),
        name="adaln",
    )(cond, w_mod, b_mod.reshape(depth, 1, n))


class _Geom:
    def __init__(self, n_p, len_p, n_s, len_s):
        self.n_p, self.len_p, self.n_s, self.len_s = n_p, len_p, n_s, len_s
        self.t_p = n_p * len_p
        self.t_s = n_s * len_s
        self.t = self.t_p + self.t_s
        self.tm = math.gcd(math.gcd(self.t_p, len_s), MAX_TOKEN_TILE)

    def mod_row(self, i, tile):
        npt = self.t_p // tile
        per_seq = self.len_s // tile
        return jnp.where(i < npt, 0, 1 + (i - npt) // per_seq)


def _norm_mod_prologue(x_ref, mod_ref, g_ref, h_ref, shift_row, row_chunk):
    @pl.when(pl.program_id(1) == 0)
    def _():
        shift = mod_ref[0, shift_row:shift_row + 1, :]
        scale1 = 1.0 + mod_ref[0, shift_row + 1:shift_row + 2, :]
        g = g_ref[...]

        def body(r, carry):
            rows = pl.ds(pl.multiple_of(r * row_chunk, row_chunk), row_chunk)
            x = x_ref[rows, :]
            ms = jnp.mean(x * x, axis=-1, keepdims=True)
            y = x * lax.rsqrt(ms + EPS) * g
            h_ref[rows, :] = (y * scale1 + shift).astype(BF16)
            return carry

        lax.fori_loop(0, x_ref.shape[0] // row_chunk, body, 0)


def _ssd_in_kernel(x_ref, mod_ref, g_ref, w_ref, wdt_ref, z_ref, xbc_ref, dt_ref, h_ref, *, z_blocks, row_chunk):
    _norm_mod_prologue(x_ref, mod_ref, g_ref, h_ref, 0, row_chunk)
    j = pl.program_id(1)
    h = h_ref[...]
    acc = _dot(h, w_ref[...])

    @pl.when(j < z_blocks)
    def _():
        z_ref[...] = acc.astype(z_ref.dtype)

    @pl.when(j >= z_blocks)
    def _():
        xbc_ref[...] = acc.astype(xbc_ref.dtype)

    @pl.when(j == pl.num_programs(1) - 1)
    def _():
        dt_ref[...] = _dot(h, wdt_ref[...])


def _ssd_in_proj(geom, x, mods, gain, w, d_inner, conv_dim, n_dt, tn):
    t, d = x.shape
    tm = geom.tm
    assert d_inner % tn == 0 and conv_dim % tn == 0 and (d_inner + conv_dim) % n_dt == 0
    z_blocks, xbc_blocks = d_inner // tn, conv_dim // tn
    dt_block = (d_inner + conv_dim) // n_dt
    return pl.pallas_call(
        functools.partial(_ssd_in_kernel, z_blocks=z_blocks, row_chunk=min(tm, 256)),
        grid=(t // tm, z_blocks + xbc_blocks),
        in_specs=[pl.BlockSpec((tm, d), lambda i, j: (i, 0)),
                  pl.BlockSpec((1, 6, d), lambda i, j: (geom.mod_row(i, tm), 0, 0)),
                  pl.BlockSpec((1, d), lambda i, j: (0, 0)),
                  pl.BlockSpec((d, tn), lambda i, j: (0, j)),
                  pl.BlockSpec((d, n_dt), lambda i, j: (0, dt_block))],
        out_specs=[pl.BlockSpec((tm, tn), lambda i, j: (i, jnp.minimum(j, z_blocks - 1))),
                   pl.BlockSpec((tm, tn), lambda i, j: (i, jnp.maximum(j - z_blocks, 0))),
                   pl.BlockSpec((tm, n_dt), lambda i, j: (i, 0))],
        out_shape=[jax.ShapeDtypeStruct((t, d_inner), BF16),
                   jax.ShapeDtypeStruct((t, conv_dim), BF16),
                   jax.ShapeDtypeStruct((t, n_dt), F32)],
        scratch_shapes=[pltpu.VMEM((tm, d), BF16)],
        compiler_params=_params(2),
        name="ssd_in",
    )(x, mods, gain.reshape(1, d), w, w)


def _nm_kernel(x_ref, mod_ref, g_ref, *rest, n_w, shift_row, act, row_chunk):
    w_refs = rest[:n_w]
    o_ref = rest[n_w]
    h_ref = rest[n_w + 1]
    _norm_mod_prologue(x_ref, mod_ref, g_ref, h_ref, shift_row, row_chunk)
    h = h_ref[...]
    if act == "swiglu":
        gate = _dot(h, w_refs[0][...])
        up = _dot(h, w_refs[1][...])
        o_ref[...] = (_silu(gate) * up).astype(o_ref.dtype)
    else:
        o_ref[...] = _dot(h, w_refs[0][...]).astype(o_ref.dtype)


def _norm_mod_matmul(geom, x, mods, gain, w, col_offsets, n_cols, tn, shift_row, act, out_dtype, name):
    t, d = x.shape
    tm = geom.tm
    assert n_cols % tn == 0 and all(off % tn == 0 for off in col_offsets)
    w_specs = [pl.BlockSpec((d, tn), functools.partial(lambda i, j, ob: (0, ob + j), ob=off // tn))
               for off in col_offsets]
    return pl.pallas_call(
        functools.partial(_nm_kernel, n_w=len(col_offsets), shift_row=shift_row, act=act,
                          row_chunk=min(tm, 256)),
        grid=(t // tm, n_cols // tn),
        in_specs=[pl.BlockSpec((tm, d), lambda i, j: (i, 0)),
                  pl.BlockSpec((1, 6, d), lambda i, j: (geom.mod_row(i, tm), 0, 0)),
                  pl.BlockSpec((1, d), lambda i, j: (0, 0))] + w_specs,
        out_specs=pl.BlockSpec((tm, tn), lambda i, j: (i, j)),
        out_shape=jax.ShapeDtypeStruct((t, n_cols), out_dtype),
        scratch_shapes=[pltpu.VMEM((tm, d), BF16)],
        compiler_params=_params(2),
        name=name,
    )(x, mods, gain.reshape(1, d), *([w] * len(col_offsets)))


def _gate_res_kernel(a_ref, w_ref, x_ref, mod_ref, o_ref, *, gate_row):
    acc = _dot(a_ref[...], w_ref[...])
    o_ref[...] = x_ref[...] + mod_ref[0, gate_row:gate_row + 1, :] * acc


def _matmul_gate_res(geom, a, w, x, mods, gate_row, tn, name):
    t, k = a.shape
    d = w.shape[1]
    tm = geom.tm
    assert d % tn == 0
    return pl.pallas_call(
        functools.partial(_gate_res_kernel, gate_row=gate_row),
        grid=(t // tm, d // tn),
        in_specs=[pl.BlockSpec((tm, k), lambda i, j: (i, 0)),
                  pl.BlockSpec((k, tn), lambda i, j: (0, j)),
                  pl.BlockSpec((tm, tn), lambda i, j: (i, j)),
                  pl.BlockSpec((1, 6, tn), lambda i, j: (geom.mod_row(i, tm), 0, j))],
        out_specs=pl.BlockSpec((tm, tn), lambda i, j: (i, j)),
        out_shape=jax.ShapeDtypeStruct((t, d), F32),
        compiler_params=_params(2),
        name=name,
    )(a, w, x, mods)


def _rmsnorm_kernel(x_ref, g_ref, o_ref):
    x = x_ref[...]
    ms = jnp.mean(x * x, axis=-1, keepdims=True)
    o_ref[...] = x * lax.rsqrt(ms + EPS) * g_ref[...]


def _final_norm(x, gain, row0, n_rows, name):
    d = x.shape[1]
    tr = 256
    assert row0 % tr == 0 and n_rows % tr == 0
    return pl.pallas_call(
        _rmsnorm_kernel,
        grid=(n_rows // tr,),
        in_specs=[pl.BlockSpec((tr, d), lambda i: (row0 // tr + i, 0)),
                  pl.BlockSpec((1, d), lambda i: (0, 0))],
        out_specs=pl.BlockSpec((tr, d), lambda i: (i, 0)),
        out_shape=jax.ShapeDtypeStruct((n_rows, d), F32),
        compiler_params=_params(1),
        name=name,
    )(x, gain.reshape(1, d))


def _conv_kernel(u_ref, prev_ref, next_ref, w_ref, b_ref, o_ref, ext_ref, *, n_p_blocks, p_blocks, s_blocks):
    i = pl.program_id(0)
    lb = u_ref.shape[0]
    is_p = i < n_p_blocks
    pos = jnp.where(is_p, i % p_blocks, (i - n_p_blocks) % s_blocks)
    last = jnp.where(is_p, p_blocks, s_blocks) - 1
    ext_ref[0:CONV_HALO, :] = jnp.where(pos > 0, prev_ref[...].astype(F32), 0.0)
    ext_ref[CONV_HALO:CONV_HALO + lb, :] = u_ref[...].astype(F32)
    ext_ref[CONV_HALO + lb:2 * CONV_HALO + lb, :] = jnp.where(pos < last, next_ref[...].astype(F32), 0.0)
    ext = ext_ref[...]
    n_ext = ext.shape[0]
    centre = D_CONV // 2
    acc = b_ref[...] + w_ref[centre:centre + 1, :] * ext[CONV_HALO:CONV_HALO + lb, :]
    for k in range(D_CONV):
        if k != centre:
            shifted = pltpu.roll(ext, (centre - k) % n_ext, 0)
            acc = acc + w_ref[k:k + 1, :] * shifted[CONV_HALO:CONV_HALO + lb, :]
    o_ref[...] = _silu(acc).astype(o_ref.dtype)


def _ssd_conv(geom, xbc, conv_w, conv_b):
    t, c = xbc.shape
    lb, ct = CONV_ROWS, 1024
    assert geom.len_p % lb == 0 and geom.len_s % lb == 0 and c % ct == 0
    hb = lb // CONV_HALO
    n_halo_blocks = t // CONV_HALO
    kern = functools.partial(_conv_kernel, n_p_blocks=geom.t_p // lb, p_blocks=geom.len_p // lb,
                             s_blocks=geom.len_s // lb)
    return pl.pallas_call(
        kern,
        grid=(t // lb, c // ct),
        in_specs=[pl.BlockSpec((lb, ct), lambda i, j: (i, j)),
                  pl.BlockSpec((CONV_HALO, ct), lambda i, j: (jnp.maximum(i * hb - 1, 0), j)),
                  pl.BlockSpec((CONV_HALO, ct), lambda i, j: (jnp.minimum((i + 1) * hb, n_halo_blocks - 1), j)),
                  pl.BlockSpec((D_CONV, ct), lambda i, j: (0, j)),
                  pl.BlockSpec((1, ct), lambda i, j: (0, j))],
        out_specs=pl.BlockSpec((lb, ct), lambda i, j: (i, j)),
        out_shape=jax.ShapeDtypeStruct((t, c), BF16),
        scratch_shapes=[pltpu.VMEM((lb + 2 * CONV_HALO, ct), F32)],
        compiler_params=_params(2),
        name="ssd_conv",
    )(xbc, xbc, xbc, conv_w, conv_b.reshape(1, c))


def _scan_kernel(*refs, rev, fuse_norm, n_steps, n_p_steps, p_steps, s_steps, hpg):
    if fuse_norm:
        (x_ref, b_ref, c_ref, dtc_ref, dtr_ref, pc_ref, pr_ref, s0_ref, yo_ref, z_ref, dsk_ref, ng_ref,
         y_ref, sfin_ref, st_ref) = refs
    else:
        (x_ref, b_ref, c_ref, dtc_ref, dtr_ref, pc_ref, pr_ref, s0_ref,
         y_ref, sfin_ref, st_ref) = refs
    q = CHUNK
    hd = SSD_HEAD_DIM
    width = hpg * hd
    step = pl.program_id(1)
    blk = (n_steps - 1 - step) if rev else step
    is_p = blk < n_p_steps
    pos = jnp.where(is_p, blk % p_steps, (blk - n_p_steps) % s_steps)
    last = jnp.where(is_p, p_steps, s_steps) - 1
    seq_start = (pos == last) if rev else (pos == 0)
    seq_end = (pos == 0) if rev else (pos == last)

    @pl.when(jnp.logical_and(seq_start, is_p))
    def _():
        st_ref[...] = jnp.zeros_like(st_ref)

    @pl.when(jnp.logical_and(seq_start, jnp.logical_not(is_p)))
    def _():
        st_ref[...] = s0_ref[...].T

    a_col = -jnp.exp(pc_ref[0:1, :])
    bias_col = pc_ref[1:2, :]
    a_row = -jnp.exp(pr_ref[:, 0:1])
    bias_row = pr_ref[:, 1:2]

    li = lax.broadcasted_iota(jnp.int32, (q, q), 0)
    si = lax.broadcasted_iota(jnp.int32, (q, q), 1)
    lower = li >= si
    upper = li <= si
    keep = upper if rev else lower
    tri_col = keep.astype(BF16)
    tri_row = (lower if rev else upper).astype(BF16)
    expand = (lax.broadcasted_iota(jnp.int32, (hpg, width), 1) // hd
              == lax.broadcasted_iota(jnp.int32, (hpg, width), 0)).astype(BF16)
    lane = lax.broadcasted_iota(jnp.int32, (q, 2 * hd), 1)

    def widen(v):
        hi, lo = _split2(v)
        return _dot(hi, expand) + _dot(lo, expand)

    n_chunks = x_ref.shape[0] // q
    order = range(n_chunks - 1, -1, -1) if rev else range(n_chunks)
    for ci in order:
        r0 = ci * q
        x = x_ref[r0:r0 + q, :]
        bm = b_ref[r0:r0 + q, :]
        cm = c_ref[r0:r0 + q, :]
        dt_c = _softplus(dtc_ref[r0:r0 + q, :] + bias_col)
        dt_r = _softplus(dtr_ref[:, r0:r0 + q] + bias_row)
        cs_c = sum(_dot(tri_col, part) for part in _split3(dt_c * a_col))
        cs_r = sum(_dot(part, tri_row) for part in _split3(dt_r * a_row))
        edge = 0 if rev else q - 1
        total = cs_c[edge:edge + 1, :]
        e_x = widen(jnp.exp(cs_c))
        w_x = widen(dt_c * jnp.exp(total - cs_c))

        cb = lax.dot_general(cm, bm, (((1,), (1,)), ((), ())), preferred_element_type=F32)
        state = st_ref[...]
        y = _dot(cm, state.astype(BF16)) * e_x
        parts = []
        for p in range(hpg // 2):
            ms = []
            for r in (2 * p, 2 * p + 1):
                seg = cs_c[:, r:r + 1] - cs_r[r:r + 1, :]
                dec = jnp.exp(jnp.where(keep, seg, -jnp.inf))
                ms.append((cb * dec * dt_r[r:r + 1, :]).astype(BF16))
            x2 = x[:, 2 * p * hd:2 * (p + 1) * hd]
            zero = jnp.zeros_like(x2)
            rhs = jnp.concatenate([jnp.where(lane < hd, x2, zero), jnp.where(lane >= hd, x2, zero)], axis=0)
            parts.append(_dot(jnp.concatenate(ms, axis=1), rhs))
        y = y + jnp.concatenate(parts, axis=1)

        xw = (x.astype(F32) * w_x).astype(BF16)
        st_ref[...] = state * e_x[edge:edge + 1, :] + lax.dot_general(
            bm, xw, (((0,), (0,)), ((), ())), preferred_element_type=F32)

        if fuse_norm:
            y = y + yo_ref[r0:r0 + q, :].astype(F32) + dsk_ref[...] * x.astype(F32)
            yz = y * _silu(z_ref[r0:r0 + q, :].astype(F32))
            yz = yz * lax.rsqrt(jnp.mean(yz * yz, axis=-1, keepdims=True) + EPS)
            y = yz * ng_ref[...]
        y_ref[r0:r0 + q, :] = y.astype(y_ref.dtype)

    @pl.when(jnp.logical_and(seq_end, is_p))
    def _():
        sfin_ref[...] = st_ref[...].T


def _ssd_scan(geom, direction, xbc, dt_col, dt_row, p_col, p_row, s0, y_other=None, z=None, dsk=None, ng=None):
    t = xbc.shape[0]
    g_n = SSD_GROUPS
    hpg = dt_col.shape[-1]
    width = hpg * SSD_HEAD_DIM
    n = D_STATE
    rb = SCAN_ROWS
    assert geom.len_p % rb == 0 and geom.len_s % rb == 0
    n_steps = t // rb
    n_p_steps, p_steps, s_steps = geom.t_p // rb, geom.len_p // rb, geom.len_s // rb
    rev = direction == 1
    fuse = y_other is not None
    x_cols = g_n * width
    b_blk0 = x_cols // n
    c_blk0 = (x_cols + g_n * n) // n

    def blk(s):
        return (n_steps - 1 - s) if rev else s

    def s0_idx(g, s):
        return (jnp.clip((blk(s) - n_p_steps) // s_steps, 0, geom.n_s - 1), g, 0, 0)

    def sfin_idx(g, s):
        return (jnp.minimum(blk(s) // p_steps, geom.n_p - 1), g, 0, 0)

    in_specs = [pl.BlockSpec((rb, width), lambda g, s: (blk(s), g)),
                pl.BlockSpec((rb, n), lambda g, s: (blk(s), b_blk0 + g)),
                pl.BlockSpec((rb, n), lambda g, s: (blk(s), c_blk0 + g)),
                pl.BlockSpec((None, None, rb, hpg), lambda g, s: (direction, g, blk(s), 0)),
                pl.BlockSpec((None, None, hpg, rb), lambda g, s: (direction, g, 0, blk(s))),
                pl.BlockSpec((None, None, 2, hpg), lambda g, s: (direction, g, 0, 0)),
                pl.BlockSpec((None, None, hpg, 2), lambda g, s: (direction, g, 0, 0)),
                pl.BlockSpec((None, None, width, n), s0_idx)]
    args = [xbc, xbc, xbc, dt_col, dt_row, p_col, p_row, s0]
    if fuse:
        in_specs += [pl.BlockSpec((rb, width), lambda g, s: (blk(s), g)),
                     pl.BlockSpec((rb, width), lambda g, s: (blk(s), g)),
                     pl.BlockSpec((None, 1, width), lambda g, s: (g, 0, 0)),
                     pl.BlockSpec((None, 1, width), lambda g, s: (g, 0, 0))]
        args += [y_other, z, dsk, ng]
    kern = functools.partial(_scan_kernel, rev=rev, fuse_norm=fuse, n_steps=n_steps, n_p_steps=n_p_steps,
                             p_steps=p_steps, s_steps=s_steps, hpg=hpg)
    return pl.pallas_call(
        kern,
        grid=(g_n, n_steps),
        in_specs=in_specs,
        out_specs=[pl.BlockSpec((rb, width), lambda g, s: (blk(s), g)),
                   pl.BlockSpec((None, None, width, n), sfin_idx)],
        out_shape=[jax.ShapeDtypeStruct((t, x_cols), BF16),
                   jax.ShapeDtypeStruct((geom.n_p, g_n, width, n), F32)],
        scratch_shapes=[pltpu.VMEM((n, width), F32)],
        compiler_params=_params(2),
        name="ssd_scan_bwd" if rev else "ssd_scan_fwd",
    )(*args)


def _qk_prep_kernel(qkv_ref, cos_ref, sin_ref, qg_ref, kg_ref, q_ref, k_ref, v_ref, kf_ref, vf_ref,
                    *, n_p_tiles, n_heads, n_kv):
    i = pl.program_id(0)
    hd = ATTN_HEAD_DIM
    is_p = i < n_p_tiles
    cos = jnp.where(is_p, 1.0, cos_ref[...])
    sin = jnp.where(is_p, 0.0, sin_ref[...])
    lane = lax.broadcasted_iota(jnp.int32, cos.shape, 1)
    first_half = (lane % (hd // 2)) < (hd // 4)

    def norm_rope(v, gain, cos, sin):
        n = v * lax.rsqrt(jnp.mean(v * v, axis=-1, keepdims=True) + EPS) * gain
        partner = jnp.where(first_half, pltpu.roll(n, hd - hd // 4, 1), pltpu.roll(n, hd // 4, 1))
        return n, n * cos + partner * sin

    q_scale = hd ** -0.5 * math.log2(math.e)
    cos_q = cos * q_scale
    sin_q = sin * q_scale
    for h in range(n_heads):
        _, roped = norm_rope(qkv_ref[:, h * hd:(h + 1) * hd], qg_ref[...], cos_q, sin_q)
        q_ref[:, h * hd:(h + 1) * hd] = roped.astype(q_ref.dtype)
    k0 = n_heads * hd
    v0 = (n_heads + n_kv) * hd
    for h in range(n_kv):
        normed, roped = norm_rope(qkv_ref[:, k0 + h * hd:k0 + (h + 1) * hd], kg_ref[...], cos, sin)
        k_ref[:, h * hd:(h + 1) * hd] = roped.astype(k_ref.dtype)

        @pl.when(is_p)
        def _():
            kf_ref[:, h * hd:(h + 1) * hd] = normed
    v = qkv_ref[:, v0:v0 + n_kv * hd]
    v_ref[...] = v.astype(v_ref.dtype)

    @pl.when(is_p)
    def _():
        vf_ref[...] = v


def _qk_prep(geom, qkv, cos, sin, q_gain, k_gain, n_heads):
    t = qkv.shape[0]
    hd, n_kv = ATTN_HEAD_DIM, N_KV_HEADS
    tr = 256
    assert geom.t_p % tr == 0 and geom.len_s % tr == 0
    n_p_tiles = geom.t_p // tr
    per_seq = geom.len_s // tr

    def tab_idx(i):
        return (jnp.where(i < n_p_tiles, 0, (i - n_p_tiles) % per_seq), 0)

    def ctx_idx(i):
        return (jnp.minimum(i, n_p_tiles - 1), 0)

    kern = functools.partial(_qk_prep_kernel, n_p_tiles=n_p_tiles, n_heads=n_heads, n_kv=n_kv)
    return pl.pallas_call(
        kern,
        grid=(t // tr,),
        in_specs=[pl.BlockSpec((tr, qkv.shape[1]), lambda i: (i, 0)),
                  pl.BlockSpec((tr, hd), tab_idx),
                  pl.BlockSpec((tr, hd), tab_idx),
                  pl.BlockSpec((1, hd), lambda i: (0, 0)),
                  pl.BlockSpec((1, hd), lambda i: (0, 0))],
        out_specs=[pl.BlockSpec((tr, n_heads * hd), lambda i: (i, 0)),
                   pl.BlockSpec((tr, n_kv * hd), lambda i: (i, 0)),
                   pl.BlockSpec((tr, n_kv * hd), lambda i: (i, 0)),
                   pl.BlockSpec((tr, n_kv * hd), ctx_idx),
                   pl.BlockSpec((tr, n_kv * hd), ctx_idx)],
        out_shape=[jax.ShapeDtypeStruct((t, n_heads * hd), BF16),
                   jax.ShapeDtypeStruct((t, n_kv * hd), BF16),
                   jax.ShapeDtypeStruct((t, n_kv * hd), BF16),
                   jax.ShapeDtypeStruct((geom.t_p, n_kv * hd), F32),
                   jax.ShapeDtypeStruct((geom.t_p, n_kv * hd), F32)],
        compiler_params=_params(1),
        name="qk_prep",
    )(qkv, cos, sin, q_gain.reshape(1, hd), k_gain.reshape(1, hd))


def _attn_kernel(*refs, rep, aliased, heads_per_chain):
    q_ref, k_ref, v_ref = refs[0], refs[1], refs[2]
    o_ref = refs[4] if aliased else refs[3]
    hd = ATTN_HEAD_DIM
    rows = q_ref.shape[0]
    k = k_ref[...]
    v = v_ref[...]
    for c0 in range(0, rep, heads_per_chain):
        heads = range(c0, c0 + heads_per_chain)
        q = jnp.concatenate([q_ref[:, r * hd:(r + 1) * hd] for r in heads], axis=0)
        s = lax.dot_general(q, k, (((1,), (1,)), ((), ())), preferred_element_type=F32)
        p = jnp.exp2(s - jnp.max(s, axis=-1, keepdims=True))
        denom = jnp.sum(p, axis=-1, keepdims=True)
        o = _dot(p.astype(BF16), v) / denom
        for n, r in enumerate(heads):
            o_ref[:, r * hd:(r + 1) * hd] = o[n * rows:(n + 1) * rows, :].astype(o_ref.dtype)


def _attention(q, k_ctx, v_ctx, k_lat, v_lat, geom, rep):
    t, qd = q.shape
    hd, n_kv = ATTN_HEAD_DIM, N_KV_HEADS
    qb = 128
    out_shape = jax.ShapeDtypeStruct((t, qd), BF16)
    blk_q = (qb, rep * hd)
    p_blocks = geom.len_p // qb
    o = pl.pallas_call(
        functools.partial(_attn_kernel, rep=rep, aliased=False, heads_per_chain=rep),
        grid=(geom.n_p, n_kv, p_blocks),
        in_specs=[pl.BlockSpec(blk_q, lambda b, h, i: (b * p_blocks + i, h)),
                  pl.BlockSpec((geom.len_p, hd), lambda b, h, i: (b, h)),
                  pl.BlockSpec((geom.len_p, hd), lambda b, h, i: (b, h))],
        out_specs=pl.BlockSpec(blk_q, lambda b, h, i: (b * p_blocks + i, h)),
        out_shape=out_shape,
        compiler_params=_params(3),
        name="attn_ctx",
    )(q, k_ctx, v_ctx)
    s_blocks = geom.len_s // qb
    q0 = geom.t_p // qb
    l_all = k_lat.shape[1]
    return pl.pallas_call(
        functools.partial(_attn_kernel, rep=rep, aliased=True, heads_per_chain=ATTN_LAT_HEADS_PER_CHAIN),
        grid=(geom.n_s, n_kv, s_blocks),
        in_specs=[pl.BlockSpec(blk_q, lambda b, h, i: (q0 + b * s_blocks + i, h)),
                  pl.BlockSpec((None, l_all, hd), lambda b, h, i: (b, 0, h)),
                  pl.BlockSpec((None, l_all, hd), lambda b, h, i: (b, 0, h)),
                  pl.BlockSpec(memory_space=pl.ANY)],
        out_specs=pl.BlockSpec(blk_q, lambda b, h, i: (q0 + b * s_blocks + i, h)),
        out_shape=out_shape,
        input_output_aliases={3: 0},
        compiler_params=_params(3),
        name="attn_lat",
    )(q, k_lat, v_lat, o)


def _rope_tables(length):
    half = ATTN_HEAD_DIM // 2
    pos = jnp.arange(length, dtype=jnp.int32)
    row_pos = (pos // GRID_W).astype(F32)
    col_pos = (pos % GRID_W).astype(F32)
    inv = ROPE_THETA ** (-jnp.arange(0, half, 2, dtype=F32) / half)
    ang_r = row_pos[:, None] * inv[None, :]
    ang_c = col_pos[:, None] * inv[None, :]
    cos = jnp.concatenate([jnp.cos(ang_r)] * 2 + [jnp.cos(ang_c)] * 2, axis=-1)
    sin = jnp.concatenate([-jnp.sin(ang_r), jnp.sin(ang_r), -jnp.sin(ang_c), jnp.sin(ang_c)], axis=-1)
    return cos, sin


def kernel(x_prompt, x_sample, state_ssd_fwd, state_ssd_bwd, cache_k, cache_v, c, c_ctx, w_mod, b_mod, norm_mix, norm_ffn, ssd_w_in, ssd_conv_w, ssd_conv_b, ssd_a_log, ssd_dt_bias, ssd_d, ssd_norm, ssd_w_out, attn_w_qkv, attn_q_norm, attn_k_norm, attn_w_o, ffn_w_gu, ffn_w_down, final_norm):
    n_p, len_p, d = x_prompt.shape
    n_s, len_s, _ = x_sample.shape
    depth = w_mod.shape[0]
    geom = _Geom(n_p, len_p, n_s, len_s)
    d_ff = ffn_w_down.shape[1]
    d_inner = ssd_w_out.shape[1]
    heads = d_inner // SSD_HEAD_DIM
    hpg = heads // SSD_GROUPS
    gw = hpg * SSD_HEAD_DIM
    conv_dim = d_inner + 2 * SSD_GROUPS * D_STATE
    n_heads = attn_w_o.shape[1] // ATTN_HEAD_DIM
    rep = n_heads // N_KV_HEADS

    x = jnp.concatenate([x_prompt.reshape(geom.t_p, d), x_sample.reshape(geom.t_s, d)], axis=0)

    cond_rows = -(-(1 + n_s) // 8) * 8
    cond = jnp.zeros((cond_rows, d), F32).at[0].set(c_ctx).at[1:1 + n_s].set(c)
    mods_all = _adaln(cond, w_mod, b_mod).reshape(depth, cond_rows, 6, d)

    new_f, new_b, new_k, new_v = [], [], [], []
    for i in range(depth):
        mods = mods_all[i]
        j = i // 2
        if i % 2 == 0:
            w_in = ssd_w_in[j].astype(BF16)
            z, xbc, dt_raw = _ssd_in_proj(geom, x, mods, norm_mix[i], w_in, d_inner, conv_dim, 2 * heads, 512)
            xbc = _ssd_conv(geom, xbc, ssd_conv_w[j], ssd_conv_b[j])
            dt4 = dt_raw.reshape(geom.t, 2, SSD_GROUPS, hpg)
            dt_col = dt4.transpose(1, 2, 0, 3)
            dt_row = dt4.transpose(1, 2, 3, 0)
            prm = jnp.stack([ssd_a_log[j], ssd_dt_bias[j]], axis=1).reshape(2, 2, SSD_GROUPS, hpg)
            p_col = prm.transpose(0, 2, 1, 3)
            p_row = prm.transpose(0, 2, 3, 1)
            dsk = jnp.repeat(ssd_d[j, 0] + ssd_d[j, 1], SSD_HEAD_DIM).reshape(SSD_GROUPS, 1, gw)
            ng = ssd_norm[j].reshape(SSD_GROUPS, 1, gw)
            s0_f = state_ssd_fwd[:, j].reshape(n_s, SSD_GROUPS, gw, D_STATE)
            s0_b = state_ssd_bwd[:, j].reshape(n_s, SSD_GROUPS, gw, D_STATE)
            y_b, s_b = _ssd_scan(geom, 1, xbc, dt_col, dt_row, p_col, p_row, s0_b)
            y, s_f = _ssd_scan(geom, 0, xbc, dt_col, dt_row, p_col, p_row, s0_f, y_b, z, dsk, ng)
            new_f.append(s_f.reshape(n_p, heads, SSD_HEAD_DIM, D_STATE))
            new_b.append(s_b.reshape(n_p, heads, SSD_HEAD_DIM, D_STATE))
            x = _matmul_gate_res(geom, y, ssd_w_out[j].astype(BF16), x, mods, 2, 512, "ssd_out")
        else:
            w_qkv = attn_w_qkv[j].astype(BF16)
            qkv = _norm_mod_matmul(geom, x, mods, norm_mix[i], w_qkv, [0], w_qkv.shape[1], 512, 0, None, F32,
                                   "attn_qkv")
            cos, sin = _rope_tables(len_s)
            q, k, v, k_ctx, v_ctx = _qk_prep(geom, qkv, cos, sin, attn_q_norm[j], attn_k_norm[j], n_heads)
            kv_w = N_KV_HEADS * ATTN_HEAD_DIM
            k_lat = jnp.concatenate([cache_k[:, j].reshape(n_s, -1, kv_w).astype(BF16),
                                     k[geom.t_p:].reshape(n_s, len_s, kv_w)], axis=1)
            v_lat = jnp.concatenate([cache_v[:, j].reshape(n_s, -1, kv_w).astype(BF16),
                                     v[geom.t_p:].reshape(n_s, len_s, kv_w)], axis=1)
            o = _attention(q, k, v, k_lat, v_lat, geom, rep)
            new_k.append(k_ctx.reshape(n_p, len_p, N_KV_HEADS, ATTN_HEAD_DIM))
            new_v.append(v_ctx.reshape(n_p, len_p, N_KV_HEADS, ATTN_HEAD_DIM))
            x = _matmul_gate_res(geom, o, attn_w_o[j].astype(BF16), x, mods, 2, 512, "attn_out")
        w_gu = ffn_w_gu[i].astype(BF16)
        hidden = _norm_mod_matmul(geom, x, mods, norm_ffn[i], w_gu, [0, d_ff], d_ff, 512, 3, "swiglu", BF16,
                                  "ffn_up")
        x = _matmul_gate_res(geom, hidden, ffn_w_down[i].astype(BF16), x, mods, 5, 512, "ffn_down")

    y_prompt = _final_norm(x, final_norm, 0, geom.t_p, "final_norm_ctx").reshape(n_p, len_p, d)
    y_sample = _final_norm(x, final_norm, geom.t_p, geom.t_s, "final_norm_lat").reshape(n_s, len_s, d)
    return (y_prompt, y_sample, jnp.stack(new_f, axis=1), jnp.stack(new_b, axis=1),
            jnp.stack(new_k, axis=1), jnp.stack(new_v, axis=1))
```

```python
import functools
import math

import jax
import jax.numpy as jnp
from jax import lax
from jax.experimental import pallas as pl
from jax.experimental.pallas import tpu as pltpu

F32 = jnp.float32
BF16 = jnp.bfloat16

EPS = 1e-6
GRID_W = 64
SSD_HEAD_DIM = 64
SSD_GROUPS = 8
D_STATE = 128
D_CONV = 5
CHUNK = 128
ATTN_HEAD_DIM = 128
N_KV_HEADS = 4
ROPE_THETA = 10000.0

VMEM_LIMIT_BYTES = 56 * 1024 * 1024
MAX_TOKEN_TILE = 1024
SCAN_ROWS = 2 * CHUNK
CONV_ROWS = 256
CONV_HALO = 16
ATTN_Q_ROWS = 256
ATTN_HEADS_PER_CHAIN = 1
SCAN_GROUPS_PER_STEP = 2
LOG2E = math.log2(math.e)


def _params(n_axes):
    return pltpu.CompilerParams(dimension_semantics=("arbitrary",) * n_axes,
                                vmem_limit_bytes=VMEM_LIMIT_BYTES)


def _silu(v):
    return v * jax.nn.sigmoid(v)


def _softplus(v):
    return jnp.maximum(v, 0.0) + jnp.log1p(jnp.exp(-jnp.abs(v)))


def _dot(a, b):
    return jnp.dot(a, b, preferred_element_type=F32)


def _split2(v):
    hi = v.astype(BF16)
    lo = (v - hi.astype(F32)).astype(BF16)
    return hi, lo


def _split3(v):
    hi = v.astype(BF16)
    r = v - hi.astype(F32)
    mid = r.astype(BF16)
    lo = (r - mid.astype(F32)).astype(BF16)
    return hi, mid, lo


def _adaln_kernel(c_ref, w_ref, b_ref, o_ref):
    s = _silu(c_ref[...]).astype(BF16)
    o_ref[0] = _dot(s, w_ref[0].astype(BF16)) + b_ref[0]


def _adaln(cond, w_mod, b_mod):
    depth, d, n = w_mod.shape
    rows = cond.shape[0]
    tn = 1536
    assert n % tn == 0
    return pl.pallas_call(
        _adaln_kernel,
        grid=(depth, n // tn),
        in_specs=[pl.BlockSpec((rows, d), lambda l, j: (0, 0)),
                  pl.BlockSpec((1, d, tn), lambda l, j: (l, 0, j)),
                  pl.BlockSpec((1, 1, tn), lambda l, j: (l, 0, j))],
        out_specs=pl.BlockSpec((1, rows, tn), lambda l, j: (l, 0, j)),
        out_shape=jax.ShapeDtypeStruct((depth, rows, n), F32),
        compiler_params=_params(2),
        name="adaln",
    )(cond, w_mod, b_mod.reshape(depth, 1, n))


class _Geom:
    def __init__(self, n_p, len_p, n_s, len_s):
        self.n_p, self.len_p, self.n_s, self.len_s = n_p, len_p, n_s, len_s
        self.t_p = n_p * len_p
        self.t_s = n_s * len_s
        self.t = self.t_p + self.t_s
        self.tm = math.gcd(math.gcd(self.t_p, len_s), MAX_TOKEN_TILE)

    def mod_row(self, i, tile):
        npt = self.t_p // tile
        per_seq = self.len_s // tile
        return jnp.where(i < npt, 0, 1 + (i - npt) // per_seq)


def _norm_mod_prologue(x_ref, mod_ref, g_ref, h_ref, shift_row, row_chunk):
    @pl.when(pl.program_id(1) == 0)
    def _():
        shift = mod_ref[0, shift_row:shift_row + 1, :]
        scale1 = 1.0 + mod_ref[0, shift_row + 1:shift_row + 2, :]
        g = g_ref[...]

        def body(r, carry):
            rows = pl.ds(pl.multiple_of(r * row_chunk, row_chunk), row_chunk)
            x = x_ref[rows, :]
            ms = jnp.mean(x * x, axis=-1, keepdims=True)
            y = x * lax.rsqrt(ms + EPS) * g
            h_ref[rows, :] = (y * scale1 + shift).astype(BF16)
            return carry

        lax.fori_loop(0, x_ref.shape[0] // row_chunk, body, 0)


def _ssd_in_kernel(x_ref, mod_ref, g_ref, w_ref, wdt_ref, z_ref, xbc_ref, dt_ref, h_ref, *, z_blocks, row_chunk):
    _norm_mod_prologue(x_ref, mod_ref, g_ref, h_ref, 0, row_chunk)
    j = pl.program_id(1)
    h = h_ref[...]
    acc = _dot(h, w_ref[...])

    @pl.when(j < z_blocks)
    def _():
        z_ref[...] = acc.astype(z_ref.dtype)

    @pl.when(j >= z_blocks)
    def _():
        xbc_ref[...] = acc.astype(xbc_ref.dtype)

    @pl.when(j == pl.num_programs(1) - 1)
    def _():
        dt_ref[...] = _dot(h, wdt_ref[...])


def _ssd_in_proj(geom, x, mods, gain, w, d_inner, conv_dim, n_dt, tn):
    t, d = x.shape
    tm = geom.tm
    assert d_inner % tn == 0 and conv_dim % tn == 0 and (d_inner + conv_dim) % n_dt == 0
    z_blocks, xbc_blocks = d_inner // tn, conv_dim // tn
    dt_block = (d_inner + conv_dim) // n_dt
    return pl.pallas_call(
        functools.partial(_ssd_in_kernel, z_blocks=z_blocks, row_chunk=min(tm, 256)),
        grid=(t // tm, z_blocks + xbc_blocks),
        in_specs=[pl.BlockSpec((tm, d), lambda i, j: (i, 0)),
                  pl.BlockSpec((1, 6, d), lambda i, j: (geom.mod_row(i, tm), 0, 0)),
                  pl.BlockSpec((1, d), lambda i, j: (0, 0)),
                  pl.BlockSpec((d, tn), lambda i, j: (0, j)),
                  pl.BlockSpec((d, n_dt), lambda i, j: (0, dt_block))],
        out_specs=[pl.BlockSpec((tm, tn), lambda i, j: (i, jnp.minimum(j, z_blocks - 1))),
                   pl.BlockSpec((tm, tn), lambda i, j: (i, jnp.maximum(j - z_blocks, 0))),
                   pl.BlockSpec((tm, n_dt), lambda i, j: (i, 0))],
        out_shape=[jax.ShapeDtypeStruct((t, d_inner), BF16),
                   jax.ShapeDtypeStruct((t, conv_dim), BF16),
                   jax.ShapeDtypeStruct((t, n_dt), F32)],
        scratch_shapes=[pltpu.VMEM((tm, d), BF16)],
        compiler_params=_params(2),
        name="ssd_in",
    )(x, mods, gain.reshape(1, d), w, w)


def _nm_kernel(x_ref, mod_ref, g_ref, *rest, n_w, shift_row, act, row_chunk):
    w_refs = rest[:n_w]
    o_ref = rest[n_w]
    h_ref = rest[n_w + 1]
    _norm_mod_prologue(x_ref, mod_ref, g_ref, h_ref, shift_row, row_chunk)
    h = h_ref[...]
    if act == "swiglu":
        gate = _dot(h, w_refs[0][...])
        up = _dot(h, w_refs[1][...])
        o_ref[...] = (_silu(gate) * up).astype(o_ref.dtype)
    else:
        o_ref[...] = _dot(h, w_refs[0][...]).astype(o_ref.dtype)


def _norm_mod_matmul(geom, x, mods, gain, w, col_offsets, n_cols, tn, shift_row, act, out_dtype, name):
    t, d = x.shape
    tm = geom.tm
    assert n_cols % tn == 0 and all(off % tn == 0 for off in col_offsets)
    w_specs = [pl.BlockSpec((d, tn), functools.partial(lambda i, j, ob: (0, ob + j), ob=off // tn))
               for off in col_offsets]
    return pl.pallas_call(
        functools.partial(_nm_kernel, n_w=len(col_offsets), shift_row=shift_row, act=act,
                          row_chunk=min(tm, 256)),
        grid=(t // tm, n_cols // tn),
        in_specs=[pl.BlockSpec((tm, d), lambda i, j: (i, 0)),
                  pl.BlockSpec((1, 6, d), lambda i, j: (geom.mod_row(i, tm), 0, 0)),
                  pl.BlockSpec((1, d), lambda i, j: (0, 0))] + w_specs,
        out_specs=pl.BlockSpec((tm, tn), lambda i, j: (i, j)),
        out_shape=jax.ShapeDtypeStruct((t, n_cols), out_dtype),
        scratch_shapes=[pltpu.VMEM((tm, d), BF16)],
        compiler_params=_params(2),
        name=name,
    )(x, mods, gain.reshape(1, d), *([w] * len(col_offsets)))


def _gate_res_kernel(a_ref, w_ref, x_ref, mod_ref, o_ref, *, gate_row):
    acc = _dot(a_ref[...], w_ref[...])
    o_ref[...] = x_ref[...] + mod_ref[0, gate_row:gate_row + 1, :] * acc


def _matmul_gate_res(geom, a, w, x, mods, gate_row, tn, name):
    t, k = a.shape
    d = w.shape[1]
    tm = geom.tm
    assert d % tn == 0
    return pl.pallas_call(
        functools.partial(_gate_res_kernel, gate_row=gate_row),
        grid=(t // tm, d // tn),
        in_specs=[pl.BlockSpec((tm, k), lambda i, j: (i, 0)),
                  pl.BlockSpec((k, tn), lambda i, j: (0, j)),
                  pl.BlockSpec((tm, tn), lambda i, j: (i, j)),
                  pl.BlockSpec((1, 6, tn), lambda i, j: (geom.mod_row(i, tm), 0, j))],
        out_specs=pl.BlockSpec((tm, tn), lambda i, j: (i, j)),
        out_shape=jax.ShapeDtypeStruct((t, d), F32),
        compiler_params=_params(2),
        name=name,
    )(a, w, x, mods)


def _rmsnorm_kernel(x_ref, g_ref, o_ref):
    x = x_ref[...]
    ms = jnp.mean(x * x, axis=-1, keepdims=True)
    o_ref[...] = x * lax.rsqrt(ms + EPS) * g_ref[...]


def _final_norm(x, gain, row0, n_rows, name):
    d = x.shape[1]
    tr = 256
    assert row0 % tr == 0 and n_rows % tr == 0
    return pl.pallas_call(
        _rmsnorm_kernel,
        grid=(n_rows // tr,),
        in_specs=[pl.BlockSpec((tr, d), lambda i: (row0 // tr + i, 0)),
                  pl.BlockSpec((1, d), lambda i: (0, 0))],
        out_specs=pl.BlockSpec((tr, d), lambda i: (i, 0)),
        out_shape=jax.ShapeDtypeStruct((n_rows, d), F32),
        compiler_params=_params(1),
        name=name,
    )(x, gain.reshape(1, d))


def _conv_kernel(u_ref, prev_ref, next_ref, w_ref, b_ref, o_ref, ext_ref, *, n_p_blocks, p_blocks, s_blocks):
    i = pl.program_id(0)
    lb = u_ref.shape[0]
    is_p = i < n_p_blocks
    pos = jnp.where(is_p, i % p_blocks, (i - n_p_blocks) % s_blocks)
    last = jnp.where(is_p, p_blocks, s_blocks) - 1
    ext_ref[0:CONV_HALO, :] = jnp.where(pos > 0, prev_ref[...].astype(F32), 0.0)
    ext_ref[CONV_HALO:CONV_HALO + lb, :] = u_ref[...].astype(F32)
    ext_ref[CONV_HALO + lb:2 * CONV_HALO + lb, :] = jnp.where(pos < last, next_ref[...].astype(F32), 0.0)
    ext = ext_ref[...]
    n_ext = ext.shape[0]
    centre = D_CONV // 2
    acc = b_ref[...] + w_ref[centre:centre + 1, :] * ext[CONV_HALO:CONV_HALO + lb, :]
    for k in range(D_CONV):
        if k != centre:
            shifted = pltpu.roll(ext, (centre - k) % n_ext, 0)
            acc = acc + w_ref[k:k + 1, :] * shifted[CONV_HALO:CONV_HALO + lb, :]
    o_ref[...] = _silu(acc).astype(o_ref.dtype)


def _ssd_conv(geom, xbc, conv_w, conv_b):
    t, c = xbc.shape
    lb, ct = CONV_ROWS, 1024
    assert geom.len_p % lb == 0 and geom.len_s % lb == 0 and c % ct == 0
    hb = lb // CONV_HALO
    n_halo_blocks = t // CONV_HALO
    kern = functools.partial(_conv_kernel, n_p_blocks=geom.t_p // lb, p_blocks=geom.len_p // lb,
                             s_blocks=geom.len_s // lb)
    return pl.pallas_call(
        kern,
        grid=(t // lb, c // ct),
        in_specs=[pl.BlockSpec((lb, ct), lambda i, j: (i, j)),
                  pl.BlockSpec((CONV_HALO, ct), lambda i, j: (jnp.maximum(i * hb - 1, 0), j)),
                  pl.BlockSpec((CONV_HALO, ct), lambda i, j: (jnp.minimum((i + 1) * hb, n_halo_blocks - 1), j)),
                  pl.BlockSpec((D_CONV, ct), lambda i, j: (0, j)),
                  pl.BlockSpec((1, ct), lambda i, j: (0, j))],
        out_specs=pl.BlockSpec((lb, ct), lambda i, j: (i, j)),
        out_shape=jax.ShapeDtypeStruct((t, c), BF16),
        scratch_shapes=[pltpu.VMEM((lb + 2 * CONV_HALO, ct), F32)],
        compiler_params=_params(2),
        name="ssd_conv",
    )(xbc, xbc, xbc, conv_w, conv_b.reshape(1, c))


def _scan_kernel(*refs, rev, fuse_norm, n_steps, n_p_steps, p_steps, s_steps, hpg, gps):
    if fuse_norm:
        (x_ref, b_ref, c_ref, dtc_ref, dtr_ref, pc_ref, pr_ref, s0_ref, yo_ref, z_ref, dsk_ref, ng_ref,
         y_ref, sfin_ref, st_ref) = refs
    else:
        (x_ref, b_ref, c_ref, dtc_ref, dtr_ref, pc_ref, pr_ref, s0_ref,
         y_ref, sfin_ref, st_ref) = refs
    q = CHUNK
    hd = SSD_HEAD_DIM
    n = D_STATE
    width = hpg * hd
    step = pl.program_id(1)
    blk = (n_steps - 1 - step) if rev else step
    is_p = blk < n_p_steps
    pos = jnp.where(is_p, blk % p_steps, (blk - n_p_steps) % s_steps)
    last = jnp.where(is_p, p_steps, s_steps) - 1
    seq_start = (pos == last) if rev else (pos == 0)
    seq_end = (pos == 0) if rev else (pos == last)

    @pl.when(jnp.logical_and(seq_start, is_p))
    def _():
        st_ref[...] = jnp.zeros_like(st_ref)

    @pl.when(jnp.logical_and(seq_start, jnp.logical_not(is_p)))
    def _():
        for gi in range(gps):
            st_ref[gi] = s0_ref[gi].T

    li = lax.broadcasted_iota(jnp.int32, (q, q), 0)
    si = lax.broadcasted_iota(jnp.int32, (q, q), 1)
    lower = li >= si
    upper = li <= si
    keep = upper if rev else lower
    tri_col = keep.astype(BF16)
    tri_row = (lower if rev else upper).astype(BF16)
    expand = (lax.broadcasted_iota(jnp.int32, (hpg, width), 1) // hd
              == lax.broadcasted_iota(jnp.int32, (hpg, width), 0)).astype(BF16)
    lane = lax.broadcasted_iota(jnp.int32, (q, 2 * hd), 1)
    edge = 0 if rev else q - 1

    def widen(v):
        hi, lo = _split2(v)
        return _dot(hi, expand) + _dot(lo, expand)

    def chunk(gi, r0):
        cols = slice(gi * width, (gi + 1) * width)
        x = x_ref[r0:r0 + q, cols]
        bm = b_ref[r0:r0 + q, gi * n:(gi + 1) * n]
        cm = c_ref[r0:r0 + q, gi * n:(gi + 1) * n]
        a_col = -jnp.exp(pc_ref[gi, 0:1, :]) * LOG2E
        a_row = -jnp.exp(pr_ref[gi, :, 0:1]) * LOG2E
        dt_c = _softplus(dtc_ref[gi, r0:r0 + q, :] + pc_ref[gi, 1:2, :])
        dt_r = _softplus(dtr_ref[gi, :, r0:r0 + q] + pr_ref[gi, :, 1:2])
        cs_c = sum(_dot(tri_col, part) for part in _split3(dt_c * a_col))
        cs_r = sum(_dot(part, tri_row) for part in _split3(dt_r * a_row))
        total = cs_c[edge:edge + 1, :]
        e_x = widen(jnp.exp2(cs_c))
        w_x = widen(dt_c * jnp.exp2(total - cs_c))
        src_r = cs_r - jnp.log2(dt_r)

        cb = lax.dot_general(cm, bm, (((1,), (1,)), ((), ())), preferred_element_type=F32)
        state = st_ref[gi]
        y = _dot(cm, state.astype(BF16)) * e_x
        parts = []
        for p in range(hpg // 2):
            ms = []
            for r in (2 * p, 2 * p + 1):
                seg = cs_c[:, r:r + 1] - src_r[r:r + 1, :]
                ms.append((cb * jnp.exp2(jnp.where(keep, seg, -jnp.inf))).astype(BF16))
            x2 = x[:, 2 * p * hd:2 * (p + 1) * hd]
            zero = jnp.zeros_like(x2)
            rhs = jnp.concatenate([jnp.where(lane < hd, x2, zero), jnp.where(lane >= hd, x2, zero)], axis=0)
            parts.append(_dot(jnp.concatenate(ms, axis=1), rhs))
        y = y + jnp.concatenate(parts, axis=1)

        xf = x.astype(F32)
        xw = (xf * w_x).astype(BF16)
        st_ref[gi] = state * e_x[edge:edge + 1, :] + lax.dot_general(
            bm, xw, (((0,), (0,)), ((), ())), preferred_element_type=F32)

        if fuse_norm:
            y = y + yo_ref[r0:r0 + q, cols].astype(F32) + dsk_ref[gi] * xf
            yz = y * _silu(z_ref[r0:r0 + q, cols].astype(F32))
            yz = yz * lax.rsqrt(jnp.mean(yz * yz, axis=-1, keepdims=True) + EPS)
            y = yz * ng_ref[gi]
        y_ref[r0:r0 + q, cols] = y.astype(y_ref.dtype)

    n_chunks = x_ref.shape[0] // q
    for ci in (range(n_chunks - 1, -1, -1) if rev else range(n_chunks)):
        for gi in range(gps):
            chunk(gi, ci * q)

    @pl.when(jnp.logical_and(seq_end, is_p))
    def _():
        for gi in range(gps):
            sfin_ref[gi] = st_ref[gi].T


def _ssd_scan(geom, direction, xbc, dt_col, dt_row, p_col, p_row, s0, y_other=None, z=None, dsk=None, ng=None):
    t = xbc.shape[0]
    g_n = SSD_GROUPS
    gps = SCAN_GROUPS_PER_STEP
    hpg = dt_col.shape[-1]
    width = hpg * SSD_HEAD_DIM
    n = D_STATE
    rb = SCAN_ROWS
    assert geom.len_p % rb == 0 and geom.len_s % rb == 0 and g_n % gps == 0
    n_steps = t // rb
    n_p_steps, p_steps, s_steps = geom.t_p // rb, geom.len_p // rb, geom.len_s // rb
    rev = direction == 1
    fuse = y_other is not None
    x_cols = g_n * width
    b_blk0 = x_cols // (gps * n)
    c_blk0 = (x_cols + g_n * n) // (gps * n)

    def blk(s):
        return (n_steps - 1 - s) if rev else s

    def s0_idx(g, s):
        return (jnp.clip((blk(s) - n_p_steps) // s_steps, 0, geom.n_s - 1), g, 0, 0)

    def sfin_idx(g, s):
        return (jnp.minimum(blk(s) // p_steps, geom.n_p - 1), g, 0, 0)

    wide = pl.BlockSpec((rb, gps * width), lambda g, s: (blk(s), g))
    in_specs = [wide,
                pl.BlockSpec((rb, gps * n), lambda g, s: (blk(s), b_blk0 + g)),
                pl.BlockSpec((rb, gps * n), lambda g, s: (blk(s), c_blk0 + g)),
                pl.BlockSpec((None, gps, rb, hpg), lambda g, s: (direction, g, blk(s), 0)),
                pl.BlockSpec((None, gps, hpg, rb), lambda g, s: (direction, g, 0, blk(s))),
                pl.BlockSpec((None, gps, 2, hpg), lambda g, s: (direction, g, 0, 0)),
                pl.BlockSpec((None, gps, hpg, 2), lambda g, s: (direction, g, 0, 0)),
                pl.BlockSpec((None, gps, width, n), s0_idx)]
    args = [xbc, xbc, xbc, dt_col, dt_row, p_col, p_row, s0]
    if fuse:
        in_specs += [wide, wide,
                     pl.BlockSpec((gps, 1, width), lambda g, s: (g, 0, 0)),
                     pl.BlockSpec((gps, 1, width), lambda g, s: (g, 0, 0))]
        args += [y_other, z, dsk, ng]
    kern = functools.partial(_scan_kernel, rev=rev, fuse_norm=fuse, n_steps=n_steps, n_p_steps=n_p_steps,
                             p_steps=p_steps, s_steps=s_steps, hpg=hpg, gps=gps)
    return pl.pallas_call(
        kern,
        grid=(g_n // gps, n_steps),
        in_specs=in_specs,
        out_specs=[wide, pl.BlockSpec((None, gps, width, n), sfin_idx)],
        out_shape=[jax.ShapeDtypeStruct((t, x_cols), BF16),
                   jax.ShapeDtypeStruct((geom.n_p, g_n, width, n), F32)],
        scratch_shapes=[pltpu.VMEM((gps, n, width), F32)],
        compiler_params=_params(2),
        name="ssd_scan_bwd" if rev else "ssd_scan_fwd",
    )(*args)


def _attn_qkv_kernel(x_ref, mod_ref, g_ref, w_ref, cos_ref, sin_ref, qg_ref, kg_ref,
                     q_ref, kc_ref, vc_ref, kl_ref, vl_ref, kf_ref, vf_ref, h_ref,
                     *, n_p_tiles, q_blocks, row_chunk):
    _norm_mod_prologue(x_ref, mod_ref, g_ref, h_ref, 0, row_chunk)
    i = pl.program_id(0)
    j = pl.program_id(1)
    hd = ATTN_HEAD_DIM
    is_p = i < n_p_tiles
    acc = _dot(h_ref[...], w_ref[...])
    heads = acc.shape[1] // hd
    cos = jnp.where(is_p, 1.0, cos_ref[...])
    sin = jnp.where(is_p, 0.0, sin_ref[...])
    lane = lax.broadcasted_iota(jnp.int32, cos.shape, 1)
    first_half = (lane % (hd // 2)) < (hd // 4)

    def norm(v, gain):
        return v * lax.rsqrt(jnp.mean(v * v, axis=-1, keepdims=True) + EPS) * gain

    def rope(v, c, s):
        partner = jnp.where(first_half, pltpu.roll(v, hd - hd // 4, 1), pltpu.roll(v, hd // 4, 1))
        return v * c + partner * s

    @pl.when(j < q_blocks)
    def _():
        q_scale = hd ** -0.5 * LOG2E
        cos_q = cos * q_scale
        sin_q = sin * q_scale
        for h in range(heads):
            cols = slice(h * hd, (h + 1) * hd)
            q_ref[:, cols] = rope(norm(acc[:, cols], qg_ref[...]), cos_q, sin_q).astype(q_ref.dtype)

    @pl.when(j == q_blocks)
    def _():
        for h in range(heads):
            cols = slice(h * hd, (h + 1) * hd)
            normed = norm(acc[:, cols], kg_ref[...])

            @pl.when(is_p)
            def _():
                kf_ref[:, cols] = normed
                kc_ref[:, cols] = normed.astype(kc_ref.dtype)

            @pl.when(jnp.logical_not(is_p))
            def _():
                kl_ref[:, cols] = rope(normed, cos, sin).astype(kl_ref.dtype)

    @pl.when(j == q_blocks + 1)
    def _():
        @pl.when(is_p)
        def _():
            vf_ref[...] = acc
            vc_ref[...] = acc.astype(vc_ref.dtype)

        @pl.when(jnp.logical_not(is_p))
        def _():
            vl_ref[...] = acc.astype(vl_ref.dtype)


def _attn_qkv(geom, x, mods, gain, w, cos, sin, q_gain, k_gain, n_heads, past):
    t, d = x.shape
    tm = geom.tm
    hd, n_kv = ATTN_HEAD_DIM, N_KV_HEADS
    tn = n_kv * hd
    q_cols = n_heads * hd
    assert q_cols % tn == 0 and w.shape[1] == q_cols + 2 * tn
    q_blocks = q_cols // tn
    npt = geom.t_p // tm
    per_seq = geom.len_s // tm

    def tab_idx(i, j):
        return (jnp.where(i < npt, 0, (i - npt) % per_seq), 0)

    def ctx_idx(i, j):
        return (jnp.minimum(i, npt - 1), 0)

    def lat_idx(i, j):
        return (jnp.clip((i - npt) // per_seq, 0, geom.n_s - 1), jnp.where(i < npt, 0, (i - npt) % per_seq), 0)

    kv_lat = jax.ShapeDtypeStruct((geom.n_s, geom.len_s + past, tn), BF16)
    return pl.pallas_call(
        functools.partial(_attn_qkv_kernel, n_p_tiles=npt, q_blocks=q_blocks, row_chunk=min(tm, 256)),
        grid=(t // tm, q_blocks + 2),
        in_specs=[pl.BlockSpec((tm, d), lambda i, j: (i, 0)),
                  pl.BlockSpec((1, 6, d), lambda i, j: (geom.mod_row(i, tm), 0, 0)),
                  pl.BlockSpec((1, d), lambda i, j: (0, 0)),
                  pl.BlockSpec((d, tn), lambda i, j: (0, j)),
                  pl.BlockSpec((tm, hd), tab_idx),
                  pl.BlockSpec((tm, hd), tab_idx),
                  pl.BlockSpec((1, hd), lambda i, j: (0, 0)),
                  pl.BlockSpec((1, hd), lambda i, j: (0, 0))],
        out_specs=[pl.BlockSpec((tm, tn), lambda i, j: (i, jnp.minimum(j, q_blocks - 1))),
                   pl.BlockSpec((tm, tn), ctx_idx),
                   pl.BlockSpec((tm, tn), ctx_idx),
                   pl.BlockSpec((None, tm, tn), lat_idx),
                   pl.BlockSpec((None, tm, tn), lat_idx),
                   pl.BlockSpec((tm, tn), ctx_idx),
                   pl.BlockSpec((tm, tn), ctx_idx)],
        out_shape=[jax.ShapeDtypeStruct((t, q_cols), BF16),
                   jax.ShapeDtypeStruct((geom.t_p, tn), BF16),
                   jax.ShapeDtypeStruct((geom.t_p, tn), BF16),
                   kv_lat, kv_lat,
                   jax.ShapeDtypeStruct((geom.t_p, tn), F32),
                   jax.ShapeDtypeStruct((geom.t_p, tn), F32)],
        scratch_shapes=[pltpu.VMEM((tm, d), BF16)],
        compiler_params=_params(2),
        name="attn_qkv",
    )(x, mods, gain.reshape(1, d), w, cos, sin, q_gain.reshape(1, hd), k_gain.reshape(1, hd))


def _append_cache_kernel(ck_ref, cv_ref, k_in, v_in, k_ref, v_ref):
    k_ref[...] = ck_ref[...].astype(k_ref.dtype)
    v_ref[...] = cv_ref[...].astype(v_ref.dtype)


def _append_cache(k_lat, v_lat, cache_k, cache_v, len_s):
    n_s, past, kv = cache_k.shape
    assert len_s % past == 0
    cache_spec = pl.BlockSpec((None, past, kv), lambda b: (b, 0, 0))
    tail_spec = pl.BlockSpec((None, past, kv), lambda b: (b, len_s // past, 0))
    any_spec = pl.BlockSpec(memory_space=pl.ANY)
    return pl.pallas_call(
        _append_cache_kernel,
        grid=(n_s,),
        in_specs=[cache_spec, cache_spec, any_spec, any_spec],
        out_specs=[tail_spec, tail_spec],
        out_shape=[jax.ShapeDtypeStruct(k_lat.shape, k_lat.dtype)] * 2,
        input_output_aliases={2: 0, 3: 1},
        compiler_params=_params(1),
        name="append_cache",
    )(cache_k, cache_v, k_lat, v_lat)


def _attn_kernel(*refs, rep, aliased, heads_per_chain):
    q_ref, k_ref, v_ref = refs[0], refs[1], refs[2]
    o_ref = refs[4] if aliased else refs[3]
    hd = ATTN_HEAD_DIM
    rows = q_ref.shape[0]
    k = k_ref[...]
    v = v_ref[...]
    for c0 in range(0, rep, heads_per_chain):
        heads = range(c0, c0 + heads_per_chain)
        q = jnp.concatenate([q_ref[:, r * hd:(r + 1) * hd] for r in heads], axis=0)
        s = lax.dot_general(q, k, (((1,), (1,)), ((), ())), preferred_element_type=F32)
        p = jnp.exp2(s - jnp.max(s, axis=-1, keepdims=True))
        denom = jnp.sum(p, axis=-1, keepdims=True)
        o = _dot(p.astype(BF16), v) / denom
        for n, r in enumerate(heads):
            o_ref[:, r * hd:(r + 1) * hd] = o[n * rows:(n + 1) * rows, :].astype(o_ref.dtype)


def _attention(q, k_ctx, v_ctx, k_lat, v_lat, geom, rep):
    t, qd = q.shape
    hd, n_kv = ATTN_HEAD_DIM, N_KV_HEADS
    qb = ATTN_Q_ROWS
    assert geom.len_p % qb == 0 and geom.len_s % qb == 0 and rep % ATTN_HEADS_PER_CHAIN == 0
    out_shape = jax.ShapeDtypeStruct((t, qd), BF16)
    blk_q = (qb, rep * hd)
    p_blocks = geom.len_p // qb
    o = pl.pallas_call(
        functools.partial(_attn_kernel, rep=rep, aliased=False, heads_per_chain=rep),
        grid=(geom.n_p, n_kv, p_blocks),
        in_specs=[pl.BlockSpec(blk_q, lambda b, h, i: (b * p_blocks + i, h)),
                  pl.BlockSpec((geom.len_p, hd), lambda b, h, i: (b, h)),
                  pl.BlockSpec((geom.len_p, hd), lambda b, h, i: (b, h))],
        out_specs=pl.BlockSpec(blk_q, lambda b, h, i: (b * p_blocks + i, h)),
        out_shape=out_shape,
        compiler_params=_params(3),
        name="attn_ctx",
    )(q, k_ctx, v_ctx)
    s_blocks = geom.len_s // qb
    q0 = geom.t_p // qb
    l_all = k_lat.shape[1]
    return pl.pallas_call(
        functools.partial(_attn_kernel, rep=rep, aliased=True, heads_per_chain=ATTN_HEADS_PER_CHAIN),
        grid=(geom.n_s, n_kv, s_blocks),
        in_specs=[pl.BlockSpec(blk_q, lambda b, h, i: (q0 + b * s_blocks + i, h)),
                  pl.BlockSpec((None, l_all, hd), lambda b, h, i: (b, 0, h)),
                  pl.BlockSpec((None, l_all, hd), lambda b, h, i: (b, 0, h)),
                  pl.BlockSpec(memory_space=pl.ANY)],
        out_specs=pl.BlockSpec(blk_q, lambda b, h, i: (q0 + b * s_blocks + i, h)),
        out_shape=out_shape,
        input_output_aliases={3: 0},
        compiler_params=_params(3),
        name="attn_lat",
    )(q, k_lat, v_lat, o)


def _rope_tables(length):
    half = ATTN_HEAD_DIM // 2
    pos = jnp.arange(length, dtype=jnp.int32)
    row_pos = (pos // GRID_W).astype(F32)
    col_pos = (pos % GRID_W).astype(F32)
    inv = ROPE_THETA ** (-jnp.arange(0, half, 2, dtype=F32) / half)
    ang_r = row_pos[:, None] * inv[None, :]
    ang_c = col_pos[:, None] * inv[None, :]
    cos = jnp.concatenate([jnp.cos(ang_r)] * 2 + [jnp.cos(ang_c)] * 2, axis=-1)
    sin = jnp.concatenate([-jnp.sin(ang_r), jnp.sin(ang_r), -jnp.sin(ang_c), jnp.sin(ang_c)], axis=-1)
    return cos, sin


def kernel(x_prompt, x_sample, state_ssd_fwd, state_ssd_bwd, cache_k, cache_v, c, c_ctx, w_mod, b_mod, norm_mix, norm_ffn, ssd_w_in, ssd_conv_w, ssd_conv_b, ssd_a_log, ssd_dt_bias, ssd_d, ssd_norm, ssd_w_out, attn_w_qkv, attn_q_norm, attn_k_norm, attn_w_o, ffn_w_gu, ffn_w_down, final_norm):
    n_p, len_p, d = x_prompt.shape
    n_s, len_s, _ = x_sample.shape
    depth = w_mod.shape[0]
    geom = _Geom(n_p, len_p, n_s, len_s)
    d_ff = ffn_w_down.shape[1]
    d_inner = ssd_w_out.shape[1]
    heads = d_inner // SSD_HEAD_DIM
    hpg = heads // SSD_GROUPS
    gw = hpg * SSD_HEAD_DIM
    conv_dim = d_inner + 2 * SSD_GROUPS * D_STATE
    n_heads = attn_w_o.shape[1] // ATTN_HEAD_DIM
    rep = n_heads // N_KV_HEADS

    x = jnp.concatenate([x_prompt.reshape(geom.t_p, d), x_sample.reshape(geom.t_s, d)], axis=0)

    cond_rows = -(-(1 + n_s) // 8) * 8
    cond = jnp.zeros((cond_rows, d), F32).at[0].set(c_ctx).at[1:1 + n_s].set(c)
    mods_all = _adaln(cond, w_mod, b_mod).reshape(depth, cond_rows, 6, d)

    new_f, new_b, new_k, new_v = [], [], [], []
    for i in range(depth):
        mods = mods_all[i]
        j = i // 2
        if i % 2 == 0:
            w_in = ssd_w_in[j].astype(BF16)
            z, xbc, dt_raw = _ssd_in_proj(geom, x, mods, norm_mix[i], w_in, d_inner, conv_dim, 2 * heads, 1024)
            xbc = _ssd_conv(geom, xbc, ssd_conv_w[j], ssd_conv_b[j])
            dt4 = dt_raw.reshape(geom.t, 2, SSD_GROUPS, hpg)
            dt_col = dt4.transpose(1, 2, 0, 3)
            dt_row = dt4.transpose(1, 2, 3, 0)
            prm = jnp.stack([ssd_a_log[j], ssd_dt_bias[j]], axis=1).reshape(2, 2, SSD_GROUPS, hpg)
            p_col = prm.transpose(0, 2, 1, 3)
            p_row = prm.transpose(0, 2, 3, 1)
            dsk = jnp.repeat(ssd_d[j, 0] + ssd_d[j, 1], SSD_HEAD_DIM).reshape(SSD_GROUPS, 1, gw)
            ng = ssd_norm[j].reshape(SSD_GROUPS, 1, gw)
            s0_f = state_ssd_fwd[:, j].reshape(n_s, SSD_GROUPS, gw, D_STATE)
            s0_b = state_ssd_bwd[:, j].reshape(n_s, SSD_GROUPS, gw, D_STATE)
            y_b, s_b = _ssd_scan(geom, 1, xbc, dt_col, dt_row, p_col, p_row, s0_b)
            y, s_f = _ssd_scan(geom, 0, xbc, dt_col, dt_row, p_col, p_row, s0_f, y_b, z, dsk, ng)
            new_f.append(s_f.reshape(n_p, heads, SSD_HEAD_DIM, D_STATE))
            new_b.append(s_b.reshape(n_p, heads, SSD_HEAD_DIM, D_STATE))
            x = _matmul_gate_res(geom, y, ssd_w_out[j].astype(BF16), x, mods, 2, 512, "ssd_out")
        else:
            w_qkv = attn_w_qkv[j].astype(BF16)
            cos, sin = _rope_tables(len_s)
            kv_w = N_KV_HEADS * ATTN_HEAD_DIM
            past = cache_k.shape[2]
            q, k_c, v_c, k_lat, v_lat, k_ctx, v_ctx = _attn_qkv(
                geom, x, mods, norm_mix[i], w_qkv, cos, sin, attn_q_norm[j], attn_k_norm[j], n_heads, past)
            k_lat, v_lat = _append_cache(k_lat, v_lat, cache_k[:, j].reshape(n_s, past, kv_w),
                                         cache_v[:, j].reshape(n_s, past, kv_w), len_s)
            o = _attention(q, k_c, v_c, k_lat, v_lat, geom, rep)
            new_k.append(k_ctx.reshape(n_p, len_p, N_KV_HEADS, ATTN_HEAD_DIM))
            new_v.append(v_ctx.reshape(n_p, len_p, N_KV_HEADS, ATTN_HEAD_DIM))
            x = _matmul_gate_res(geom, o, attn_w_o[j].astype(BF16), x, mods, 2, 1024, "attn_out")
        w_gu = ffn_w_gu[i].astype(BF16)
        hidden = _norm_mod_matmul(geom, x, mods, norm_ffn[i], w_gu, [0, d_ff], d_ff, 512, 3, "swiglu", BF16,
                                  "ffn_up")
        x = _matmul_gate_res(geom, hidden, ffn_w_down[i].astype(BF16), x, mods, 5, 512, "ffn_down")

    y_prompt = _final_norm(x, final_norm, 0, geom.t_p, "final_norm_ctx").reshape(n_p, len_p, d)
    y_sample = _final_norm(x, final_norm, geom.t_p, geom.t_s, "final_norm_lat").reshape(n_s, len_s, d)
    return (y_prompt, y_sample, jnp.stack(new_f, axis=1), jnp.stack(new_b, axis=1),
            jnp.stack(new_k, axis=1), jnp.stack(new_v, axis=1))
```

```python
import functools
import math

import jax
import jax.numpy as jnp
from jax import lax
from jax.experimental import pallas as pl
from jax.experimental.pallas import tpu as pltpu

F32 = jnp.float32
BF16 = jnp.bfloat16

EPS = 1e-6
GRID_W = 64
SSD_HEAD_DIM = 64
SSD_GROUPS = 8
D_STATE = 128
D_CONV = 5
CHUNK = 128
ATTN_HEAD_DIM = 128
N_KV_HEADS = 4
ROPE_THETA = 10000.0

VMEM_LIMIT_BYTES = 56 * 1024 * 1024
MAX_TOKEN_TILE = 1024
SCAN_ROWS = 2 * CHUNK
CONV_ROWS = 256
CONV_HALO = 16
ATTN_Q_ROWS = 256
ATTN_HEADS_PER_CHAIN = 1
SCAN_GROUPS_PER_STEP = 8
LOG2E = math.log2(math.e)


def _params(n_axes):
    return pltpu.CompilerParams(dimension_semantics=("arbitrary",) * n_axes,
                                vmem_limit_bytes=VMEM_LIMIT_BYTES)


def _silu(v):
    return v * jax.nn.sigmoid(v)


def _softplus(v):
    return jnp.maximum(v, 0.0) + jnp.log1p(jnp.exp(-jnp.abs(v)))


def _dot(a, b):
    return jnp.dot(a, b, preferred_element_type=F32)


def _split2(v):
    hi = v.astype(BF16)
    lo = (v - hi.astype(F32)).astype(BF16)
    return hi, lo


def _split3(v):
    hi = v.astype(BF16)
    r = v - hi.astype(F32)
    mid = r.astype(BF16)
    lo = (r - mid.astype(F32)).astype(BF16)
    return hi, mid, lo


def _adaln_kernel(c_ref, w_ref, b_ref, o_ref):
    s = _silu(c_ref[...]).astype(BF16)
    o_ref[0] = _dot(s, w_ref[0].astype(BF16)) + b_ref[0]


def _adaln(cond, w_mod, b_mod):
    depth, d, n = w_mod.shape
    rows = cond.shape[0]
    tn = 1536
    assert n % tn == 0
    return pl.pallas_call(
        _adaln_kernel,
        grid=(depth, n // tn),
        in_specs=[pl.BlockSpec((rows, d), lambda l, j: (0, 0)),
                  pl.BlockSpec((1, d, tn), lambda l, j: (l, 0, j)),
                  pl.BlockSpec((1, 1, tn), lambda l, j: (l, 0, j))],
        out_specs=pl.BlockSpec((1, rows, tn), lambda l, j: (l, 0, j)),
        out_shape=jax.ShapeDtypeStruct((depth, rows, n), F32),
        compiler_params=_params(2),
        name="adaln",
    )(cond, w_mod, b_mod.reshape(depth, 1, n))


class _Geom:
    def __init__(self, n_p, len_p, n_s, len_s):
        self.n_p, self.len_p, self.n_s, self.len_s = n_p, len_p, n_s, len_s
        self.t_p = n_p * len_p
        self.t_s = n_s * len_s
        self.t = self.t_p + self.t_s
        self.tm = math.gcd(math.gcd(self.t_p, len_s), MAX_TOKEN_TILE)

    def mod_row(self, i, tile):
        npt = self.t_p // tile
        per_seq = self.len_s // tile
        return jnp.where(i < npt, 0, 1 + (i - npt) // per_seq)


def _norm_mod_prologue(x_ref, mod_ref, g_ref, h_ref, shift_row, row_chunk, enable=None):
    first = pl.program_id(1) == 0

    @pl.when(first if enable is None else jnp.logical_and(first, enable))
    def _():
        shift = mod_ref[0, shift_row:shift_row + 1, :]
        scale1 = 1.0 + mod_ref[0, shift_row + 1:shift_row + 2, :]
        g = g_ref[...]

        def body(r, carry):
            rows = pl.ds(pl.multiple_of(r * row_chunk, row_chunk), row_chunk)
            x = x_ref[rows, :]
            ms = jnp.mean(x * x, axis=-1, keepdims=True)
            y = x * lax.rsqrt(ms + EPS) * g
            h_ref[rows, :] = (y * scale1 + shift).astype(BF16)
            return carry

        lax.fori_loop(0, x_ref.shape[0] // row_chunk, body, 0)


def _ssd_in_kernel(mod_ref, g_ref, w_ref, wdt_ref, *rest, z_blocks, row_chunk, n_p_tiles):
    x_refs = rest[:-4]
    z_ref, xbc_ref, dt_ref, h_ref = rest[-4:]
    if len(x_refs) == 1:
        _norm_mod_prologue(x_refs[0], mod_ref, g_ref, h_ref, 0, row_chunk)
    else:
        is_p = pl.program_id(0) < n_p_tiles
        _norm_mod_prologue(x_refs[0], mod_ref, g_ref, h_ref, 0, row_chunk, is_p)
        _norm_mod_prologue(x_refs[1], mod_ref, g_ref, h_ref, 0, row_chunk, jnp.logical_not(is_p))
    j = pl.program_id(1)
    h = h_ref[...]
    acc = _dot(h, w_ref[...])

    @pl.when(j < z_blocks)
    def _():
        z_ref[...] = acc.astype(z_ref.dtype)

    @pl.when(j >= z_blocks)
    def _():
        xbc_ref[...] = acc.astype(xbc_ref.dtype)

    @pl.when(j == pl.num_programs(1) - 1)
    def _():
        dt_ref[...] = _dot(h, wdt_ref[...])


def _ssd_in_proj(geom, x, mods, gain, w, d_inner, conv_dim, n_dt, tn):
    t, d = geom.t, w.shape[0]
    tm = geom.tm
    assert d_inner % tn == 0 and conv_dim % tn == 0 and (d_inner + conv_dim) % n_dt == 0
    z_blocks, xbc_blocks = d_inner // tn, conv_dim // tn
    dt_block = (d_inner + conv_dim) // n_dt
    xs, x_specs = _residual_specs(geom, x, (tm, d), lambda j: 0, single_buffer=True)
    return pl.pallas_call(
        functools.partial(_ssd_in_kernel, z_blocks=z_blocks, row_chunk=min(tm, 256), n_p_tiles=geom.t_p // tm),
        grid=(t // tm, z_blocks + xbc_blocks),
        in_specs=[pl.BlockSpec((1, 6, d), lambda i, j: (geom.mod_row(i, tm), 0, 0)),
                  pl.BlockSpec((1, d), lambda i, j: (0, 0)),
                  pl.BlockSpec((d, tn), lambda i, j: (0, j)),
                  pl.BlockSpec((d, n_dt), lambda i, j: (0, dt_block))] + x_specs,
        out_specs=[pl.BlockSpec((tm, tn), lambda i, j: (i, jnp.minimum(j, z_blocks - 1))),
                   pl.BlockSpec((tm, tn), lambda i, j: (i, jnp.maximum(j - z_blocks, 0))),
                   pl.BlockSpec((tm, n_dt), lambda i, j: (i, 0))],
        out_shape=[jax.ShapeDtypeStruct((t, d_inner), BF16),
                   jax.ShapeDtypeStruct((t, conv_dim), BF16),
                   jax.ShapeDtypeStruct((t, n_dt), F32)],
        scratch_shapes=[pltpu.VMEM((tm, d), BF16)],
        compiler_params=_params(2),
        name="ssd_in",
    )(mods, gain.reshape(1, d), w, w, *xs)


def _weight_spec(w, layer, k, tn, col_block0):
    if w.ndim == 2:
        return pl.BlockSpec((k, tn), lambda i, j: (0, col_block0 + j))
    return pl.BlockSpec((None, k, tn), lambda i, j: (layer, 0, col_block0 + j))


def _nm_kernel(x_ref, mod_ref, g_ref, *rest, n_w, shift_row, act, row_chunk):
    w_refs = rest[:n_w]
    o_ref = rest[n_w]
    h_ref = rest[n_w + 1]
    _norm_mod_prologue(x_ref, mod_ref, g_ref, h_ref, shift_row, row_chunk)
    h = h_ref[...]
    if act == "swiglu":
        gate = _dot(h, w_refs[0][...])
        up = _dot(h, w_refs[1][...])
        o_ref[...] = (_silu(gate) * up).astype(o_ref.dtype)
    else:
        o_ref[...] = _dot(h, w_refs[0][...]).astype(o_ref.dtype)


def _norm_mod_matmul(geom, x, mods, gain, w, layer, col_offsets, n_cols, tn, shift_row, act, out_dtype, name):
    t, d = x.shape
    tm = geom.tm
    assert n_cols % tn == 0 and all(off % tn == 0 for off in col_offsets)
    w_specs = [_weight_spec(w, layer, d, tn, off // tn) for off in col_offsets]
    return pl.pallas_call(
        functools.partial(_nm_kernel, n_w=len(col_offsets), shift_row=shift_row, act=act,
                          row_chunk=min(tm, 256)),
        grid=(t // tm, n_cols // tn),
        in_specs=[pl.BlockSpec((tm, d), lambda i, j: (i, 0)),
                  pl.BlockSpec((1, 6, d), lambda i, j: (geom.mod_row(i, tm), 0, 0)),
                  pl.BlockSpec((1, d), lambda i, j: (0, 0))] + w_specs,
        out_specs=pl.BlockSpec((tm, tn), lambda i, j: (i, j)),
        out_shape=jax.ShapeDtypeStruct((t, n_cols), out_dtype),
        scratch_shapes=[pltpu.VMEM((tm, d), BF16)],
        compiler_params=_params(2),
        name=name,
    )(x, mods, gain.reshape(1, d), *([w] * len(col_offsets)))


def _gate_res_kernel(a_ref, w_ref, mod_ref, *rest, gate_row, n_p_tiles):
    x_refs, o_ref = rest[:-1], rest[-1]
    gated = mod_ref[0, gate_row:gate_row + 1, :] * _dot(a_ref[...], w_ref[...])
    if len(x_refs) == 1:
        o_ref[...] = x_refs[0][...] + gated
    else:
        is_p = pl.program_id(0) < n_p_tiles

        @pl.when(is_p)
        def _():
            o_ref[...] = x_refs[0][...] + gated

        @pl.when(jnp.logical_not(is_p))
        def _():
            o_ref[...] = x_refs[1][...] + gated


def _residual_specs(geom, x, block, col_of, single_buffer=False):
    tm = geom.tm
    npt = geom.t_p // tm
    if not isinstance(x, tuple):
        return [x], [pl.BlockSpec(block, lambda i, j: (i, col_of(j)))]
    mode = dict(pipeline_mode=pl.Buffered(1)) if single_buffer else {}
    return list(x), [pl.BlockSpec(block, lambda i, j: (jnp.minimum(i, npt - 1), col_of(j)), **mode),
                     pl.BlockSpec(block, lambda i, j: (jnp.maximum(i - npt, 0), col_of(j)), **mode)]


def _matmul_gate_res(geom, a, w, layer, x, mods, gate_row, tn, name):
    t, k = a.shape
    d = w.shape[-1]
    tm = geom.tm
    assert d % tn == 0
    xs, x_specs = _residual_specs(geom, x, (tm, tn), lambda j: j)
    return pl.pallas_call(
        functools.partial(_gate_res_kernel, gate_row=gate_row, n_p_tiles=geom.t_p // tm),
        grid=(t // tm, d // tn),
        in_specs=[pl.BlockSpec((tm, k), lambda i, j: (i, 0)),
                  _weight_spec(w, layer, k, tn, 0),
                  pl.BlockSpec((1, 6, tn), lambda i, j: (geom.mod_row(i, tm), 0, j))] + x_specs,
        out_specs=pl.BlockSpec((tm, tn), lambda i, j: (i, j)),
        out_shape=jax.ShapeDtypeStruct((t, d), F32),
        compiler_params=_params(2),
        name=name,
    )(a, w, mods, *xs)


def _rmsnorm_kernel(x_ref, g_ref, o_ref):
    x = x_ref[...]
    ms = jnp.mean(x * x, axis=-1, keepdims=True)
    o_ref[...] = x * lax.rsqrt(ms + EPS) * g_ref[...]


def _final_norm(x, gain, row0, n_rows, name):
    d = x.shape[1]
    tr = 256
    assert row0 % tr == 0 and n_rows % tr == 0
    return pl.pallas_call(
        _rmsnorm_kernel,
        grid=(n_rows // tr,),
        in_specs=[pl.BlockSpec((tr, d), lambda i: (row0 // tr + i, 0)),
                  pl.BlockSpec((1, d), lambda i: (0, 0))],
        out_specs=pl.BlockSpec((tr, d), lambda i: (i, 0)),
        out_shape=jax.ShapeDtypeStruct((n_rows, d), F32),
        compiler_params=_params(1),
        name=name,
    )(x, gain.reshape(1, d))


def _conv_kernel(u_ref, prev_ref, next_ref, w_ref, b_ref, o_ref, ext_ref, *, n_p_blocks, p_blocks, s_blocks):
    i = pl.program_id(0)
    lb = u_ref.shape[0]
    is_p = i < n_p_blocks
    pos = jnp.where(is_p, i % p_blocks, (i - n_p_blocks) % s_blocks)
    last = jnp.where(is_p, p_blocks, s_blocks) - 1
    ext_ref[0:CONV_HALO, :] = jnp.where(pos > 0, prev_ref[...].astype(F32), 0.0)
    ext_ref[CONV_HALO:CONV_HALO + lb, :] = u_ref[...].astype(F32)
    ext_ref[CONV_HALO + lb:2 * CONV_HALO + lb, :] = jnp.where(pos < last, next_ref[...].astype(F32), 0.0)
    ext = ext_ref[...]
    n_ext = ext.shape[0]
    centre = D_CONV // 2
    acc = b_ref[...] + w_ref[centre:centre + 1, :] * ext[CONV_HALO:CONV_HALO + lb, :]
    for k in range(D_CONV):
        if k != centre:
            shifted = pltpu.roll(ext, (centre - k) % n_ext, 0)
            acc = acc + w_ref[k:k + 1, :] * shifted[CONV_HALO:CONV_HALO + lb, :]
    o_ref[...] = _silu(acc).astype(o_ref.dtype)


def _ssd_conv(geom, xbc, conv_w, conv_b):
    t, c = xbc.shape
    lb, ct = CONV_ROWS, 2048
    assert geom.len_p % lb == 0 and geom.len_s % lb == 0 and c % ct == 0
    hb = lb // CONV_HALO
    n_halo_blocks = t // CONV_HALO
    kern = functools.partial(_conv_kernel, n_p_blocks=geom.t_p // lb, p_blocks=geom.len_p // lb,
                             s_blocks=geom.len_s // lb)
    return pl.pallas_call(
        kern,
        grid=(t // lb, c // ct),
        in_specs=[pl.BlockSpec((lb, ct), lambda i, j: (i, j)),
                  pl.BlockSpec((CONV_HALO, ct), lambda i, j: (jnp.maximum(i * hb - 1, 0), j)),
                  pl.BlockSpec((CONV_HALO, ct), lambda i, j: (jnp.minimum((i + 1) * hb, n_halo_blocks - 1), j)),
                  pl.BlockSpec((D_CONV, ct), lambda i, j: (0, j)),
                  pl.BlockSpec((1, ct), lambda i, j: (0, j))],
        out_specs=pl.BlockSpec((lb, ct), lambda i, j: (i, j)),
        out_shape=jax.ShapeDtypeStruct((t, c), BF16),
        scratch_shapes=[pltpu.VMEM((lb + 2 * CONV_HALO, ct), F32)],
        compiler_params=_params(2),
        name="ssd_conv",
    )(xbc, xbc, xbc, conv_w, conv_b.reshape(1, c))


def _dt_prep_kernel(dt_ref, alog_ref, bias_ref, cs_ref, src_ref, e_ref, w_ref):
    q = CHUNK
    n_dir_heads = dt_ref.shape[1]
    li = lax.broadcasted_iota(jnp.int32, (q, q), 0)
    si = lax.broadcasted_iota(jnp.int32, (q, q), 1)
    lower = (li >= si).astype(BF16)
    upper = (li <= si).astype(BF16)
    fwd = lax.broadcasted_iota(jnp.int32, (1, n_dir_heads), 1) < n_dir_heads // 2
    a_scale = -jnp.exp(alog_ref[...]) * LOG2E
    for r0 in range(0, dt_ref.shape[0], q):
        dt = _softplus(dt_ref[r0:r0 + q, :] + bias_ref[...])
        parts = _split3(dt * a_scale)
        cs = jnp.where(fwd, sum(_dot(lower, p) for p in parts), sum(_dot(upper, p) for p in parts))
        total = jnp.where(fwd, cs[q - 1:q, :], cs[0:1, :])
        cs_ref[:, r0:r0 + q] = cs.T
        src_ref[:, r0:r0 + q] = (cs - jnp.log2(dt)).T
        e_ref[:, r0:r0 + q] = jnp.exp2(cs).T
        w_ref[:, r0:r0 + q] = (dt * jnp.exp2(total - cs)).T


def _dt_prep(dt_raw, a_log, dt_bias):
    t, nh = dt_raw.shape
    rb = 1024 if t % 1024 == 0 else SCAN_ROWS
    out = jax.ShapeDtypeStruct((nh, t), F32)
    head_major = pl.BlockSpec((nh, rb), lambda i: (0, i))
    return pl.pallas_call(
        _dt_prep_kernel,
        grid=(t // rb,),
        in_specs=[pl.BlockSpec((rb, nh), lambda i: (i, 0)),
                  pl.BlockSpec((1, nh), lambda i: (0, 0)),
                  pl.BlockSpec((1, nh), lambda i: (0, 0))],
        out_specs=[head_major] * 4,
        out_shape=[out] * 4,
        compiler_params=_params(1),
        name="ssd_dt_prep",
    )(dt_raw, a_log.reshape(1, nh), dt_bias.reshape(1, nh))


def _scan_kernel(*refs, rev, fuse_norm, n_steps, n_p_steps, p_steps, s_steps, hpg, gps):
    if fuse_norm:
        (x_ref, b_ref, c_ref, cs_ref, src_ref, e_ref, w_ref, s0_ref, yo_ref, z_ref, dsk_ref, ng_ref,
         y_ref, sfin_ref, st_ref) = refs
    else:
        (x_ref, b_ref, c_ref, cs_ref, src_ref, e_ref, w_ref, s0_ref,
         y_ref, sfin_ref, st_ref) = refs
    q = CHUNK
    hd = SSD_HEAD_DIM
    n = D_STATE
    width = hpg * hd
    step = pl.program_id(1)
    blk = (n_steps - 1 - step) if rev else step
    is_p = blk < n_p_steps
    pos = jnp.where(is_p, blk % p_steps, (blk - n_p_steps) % s_steps)
    last = jnp.where(is_p, p_steps, s_steps) - 1
    seq_start = (pos == last) if rev else (pos == 0)
    seq_end = (pos == 0) if rev else (pos == last)

    @pl.when(jnp.logical_and(seq_start, is_p))
    def _():
        st_ref[...] = jnp.zeros_like(st_ref)

    @pl.when(jnp.logical_and(seq_start, jnp.logical_not(is_p)))
    def _():
        for gi in range(gps):
            st_ref[gi] = s0_ref[gi].T

    li = lax.broadcasted_iota(jnp.int32, (q, q), 0)
    si = lax.broadcasted_iota(jnp.int32, (q, q), 1)
    keep = (li <= si) if rev else (li >= si)
    expand = (lax.broadcasted_iota(jnp.int32, (2 * hpg, width), 1) // hd
              == lax.broadcasted_iota(jnp.int32, (2 * hpg, width), 0) % hpg).astype(BF16)
    lane = lax.broadcasted_iota(jnp.int32, (q, 2 * hd), 1)
    edge = 0 if rev else q - 1

    def widen(v_r):
        hi, lo = _split2(v_r)
        return lax.dot_general(jnp.concatenate([hi, lo], axis=0), expand, (((0,), (0,)), ((), ())),
                               preferred_element_type=F32)

    def chunk(gi, r0):
        cols = slice(gi * width, (gi + 1) * width)
        heads = slice(gi * hpg, (gi + 1) * hpg)
        x = x_ref[r0:r0 + q, cols]
        bm = b_ref[r0:r0 + q, gi * n:(gi + 1) * n]
        cm = c_ref[r0:r0 + q, gi * n:(gi + 1) * n]
        cs_c = cs_ref[heads, r0:r0 + q].T
        src_r = src_ref[heads, r0:r0 + q]
        e_x = widen(e_ref[heads, r0:r0 + q])
        w_x = widen(w_ref[heads, r0:r0 + q])

        cb = lax.dot_general(cm, bm, (((1,), (1,)), ((), ())), preferred_element_type=F32)
        state = st_ref[gi]
        y = _dot(cm, state.astype(BF16)) * e_x
        parts = []
        for p in range(hpg // 2):
            ms = []
            for r in (2 * p, 2 * p + 1):
                seg = cs_c[:, r:r + 1] - src_r[r:r + 1, :]
                ms.append((cb * jnp.exp2(jnp.where(keep, seg, -jnp.inf))).astype(BF16))
            x2 = x[:, 2 * p * hd:2 * (p + 1) * hd]
            zero = jnp.zeros_like(x2)
            rhs = jnp.concatenate([jnp.where(lane < hd, x2, zero), jnp.where(lane >= hd, x2, zero)], axis=0)
            parts.append(_dot(jnp.concatenate(ms, axis=1), rhs))
        y = y + jnp.concatenate(parts, axis=1)

        xf = x.astype(F32)
        xw = (xf * w_x).astype(BF16)
        st_ref[gi] = state * e_x[edge:edge + 1, :] + lax.dot_general(
            bm, xw, (((0,), (0,)), ((), ())), preferred_element_type=F32)

        if fuse_norm:
            y = y + yo_ref[r0:r0 + q, cols].astype(F32) + dsk_ref[gi] * xf
            yz = y * _silu(z_ref[r0:r0 + q, cols].astype(F32))
            yz = yz * lax.rsqrt(jnp.mean(yz * yz, axis=-1, keepdims=True) + EPS)
            y = yz * ng_ref[gi]
        y_ref[r0:r0 + q, cols] = y.astype(y_ref.dtype)

    n_chunks = x_ref.shape[0] // q
    for ci in (range(n_chunks - 1, -1, -1) if rev else range(n_chunks)):
        for gi in range(gps):
            chunk(gi, ci * q)

    @pl.when(jnp.logical_and(seq_end, is_p))
    def _():
        for gi in range(gps):
            sfin_ref[gi] = st_ref[gi].T


def _ssd_scan(geom, direction, xbc, decays, s0, y_other=None, z=None, dsk=None, ng=None):
    t = xbc.shape[0]
    g_n = SSD_GROUPS
    gps = SCAN_GROUPS_PER_STEP
    hpg = decays[0].shape[0] // (2 * g_n)
    width = hpg * SSD_HEAD_DIM
    n = D_STATE
    rb = SCAN_ROWS
    assert geom.len_p % rb == 0 and geom.len_s % rb == 0 and g_n % gps == 0
    n_steps = t // rb
    n_p_steps, p_steps, s_steps = geom.t_p // rb, geom.len_p // rb, geom.len_s // rb
    rev = direction == 1
    fuse = y_other is not None
    x_cols = g_n * width
    b_blk0 = x_cols // (gps * n)
    c_blk0 = (x_cols + g_n * n) // (gps * n)
    head_blk0 = direction * (g_n // gps)

    def blk(s):
        return (n_steps - 1 - s) if rev else s

    def s0_idx(g, s):
        return (jnp.clip((blk(s) - n_p_steps) // s_steps, 0, geom.n_s - 1), g, 0, 0)

    def sfin_idx(g, s):
        return (jnp.minimum(blk(s) // p_steps, geom.n_p - 1), g, 0, 0)

    wide = pl.BlockSpec((rb, gps * width), lambda g, s: (blk(s), g))
    head_major = pl.BlockSpec((gps * hpg, rb), lambda g, s: (head_blk0 + g, blk(s)))
    in_specs = [wide,
                pl.BlockSpec((rb, gps * n), lambda g, s: (blk(s), b_blk0 + g)),
                pl.BlockSpec((rb, gps * n), lambda g, s: (blk(s), c_blk0 + g)),
                head_major, head_major, head_major, head_major,
                pl.BlockSpec((None, gps, width, n), s0_idx)]
    args = [xbc, xbc, xbc, *decays, s0]
    if fuse:
        in_specs += [wide, wide,
                     pl.BlockSpec((gps, 1, width), lambda g, s: (g, 0, 0)),
                     pl.BlockSpec((gps, 1, width), lambda g, s: (g, 0, 0))]
        args += [y_other, z, dsk, ng]
    kern = functools.partial(_scan_kernel, rev=rev, fuse_norm=fuse, n_steps=n_steps, n_p_steps=n_p_steps,
                             p_steps=p_steps, s_steps=s_steps, hpg=hpg, gps=gps)
    return pl.pallas_call(
        kern,
        grid=(g_n // gps, n_steps),
        in_specs=in_specs,
        out_specs=[wide, pl.BlockSpec((None, gps, width, n), sfin_idx)],
        out_shape=[jax.ShapeDtypeStruct((t, x_cols), BF16),
                   jax.ShapeDtypeStruct((geom.n_p, g_n, width, n), F32)],
        scratch_shapes=[pltpu.VMEM((gps, n, width), F32)],
        compiler_params=_params(2),
        name="ssd_scan_bwd" if rev else "ssd_scan_fwd",
    )(*args)


def _attn_qkv_kernel(x_ref, mod_ref, g_ref, w_ref, cos_ref, sin_ref, qg_ref, kg_ref,
                     q_ref, kc_ref, vc_ref, kl_ref, vl_ref, kf_ref, vf_ref, h_ref,
                     *, n_p_tiles, q_blocks, row_chunk):
    _norm_mod_prologue(x_ref, mod_ref, g_ref, h_ref, 0, row_chunk)
    i = pl.program_id(0)
    j = pl.program_id(1)
    hd = ATTN_HEAD_DIM
    is_p = i < n_p_tiles
    acc = _dot(h_ref[...], w_ref[...])
    heads = acc.shape[1] // hd
    cos = jnp.where(is_p, 1.0, cos_ref[...])
    sin = jnp.where(is_p, 0.0, sin_ref[...])
    lane = lax.broadcasted_iota(jnp.int32, cos.shape, 1)
    first_half = (lane % (hd // 2)) < (hd // 4)

    def norm(v, gain):
        return v * lax.rsqrt(jnp.mean(v * v, axis=-1, keepdims=True) + EPS) * gain

    def rope(v, c, s):
        partner = jnp.where(first_half, pltpu.roll(v, hd - hd // 4, 1), pltpu.roll(v, hd // 4, 1))
        return v * c + partner * s

    @pl.when(j < q_blocks)
    def _():
        q_scale = hd ** -0.5 * LOG2E
        cos_q = cos * q_scale
        sin_q = sin * q_scale
        for h in range(heads):
            cols = slice(h * hd, (h + 1) * hd)
            q_ref[:, cols] = rope(norm(acc[:, cols], qg_ref[...]), cos_q, sin_q).astype(q_ref.dtype)

    @pl.when(j == q_blocks)
    def _():
        for h in range(heads):
            cols = slice(h * hd, (h + 1) * hd)
            normed = norm(acc[:, cols], kg_ref[...])

            @pl.when(is_p)
            def _():
                kf_ref[:, cols] = normed
                kc_ref[:, cols] = normed.astype(kc_ref.dtype)

            @pl.when(jnp.logical_not(is_p))
            def _():
                kl_ref[:, cols] = rope(normed, cos, sin).astype(kl_ref.dtype)

    @pl.when(j == q_blocks + 1)
    def _():
        @pl.when(is_p)
        def _():
            vf_ref[...] = acc
            vc_ref[...] = acc.astype(vc_ref.dtype)

        @pl.when(jnp.logical_not(is_p))
        def _():
            vl_ref[...] = acc.astype(vl_ref.dtype)


def _attn_qkv(geom, x, mods, gain, w, cos, sin, q_gain, k_gain, n_heads, past):
    t, d = x.shape
    tm = geom.tm
    hd, n_kv = ATTN_HEAD_DIM, N_KV_HEADS
    tn = n_kv * hd
    q_cols = n_heads * hd
    assert q_cols % tn == 0 and w.shape[1] == q_cols + 2 * tn
    q_blocks = q_cols // tn
    npt = geom.t_p // tm
    per_seq = geom.len_s // tm

    def tab_idx(i, j):
        return (jnp.where(i < npt, 0, (i - npt) % per_seq), 0)

    def ctx_idx(i, j):
        return (jnp.minimum(i, npt - 1), 0)

    def lat_idx(i, j):
        return (jnp.clip((i - npt) // per_seq, 0, geom.n_s - 1), jnp.where(i < npt, 0, (i - npt) % per_seq), 0)

    kv_lat = jax.ShapeDtypeStruct((geom.n_s, geom.len_s + past, tn), BF16)
    return pl.pallas_call(
        functools.partial(_attn_qkv_kernel, n_p_tiles=npt, q_blocks=q_blocks, row_chunk=min(tm, 256)),
        grid=(t // tm, q_blocks + 2),
        in_specs=[pl.BlockSpec((tm, d), lambda i, j: (i, 0)),
                  pl.BlockSpec((1, 6, d), lambda i, j: (geom.mod_row(i, tm), 0, 0)),
                  pl.BlockSpec((1, d), lambda i, j: (0, 0)),
                  pl.BlockSpec((d, tn), lambda i, j: (0, j)),
                  pl.BlockSpec((tm, hd), tab_idx),
                  pl.BlockSpec((tm, hd), tab_idx),
                  pl.BlockSpec((1, hd), lambda i, j: (0, 0)),
                  pl.BlockSpec((1, hd), lambda i, j: (0, 0))],
        out_specs=[pl.BlockSpec((tm, tn), lambda i, j: (i, jnp.minimum(j, q_blocks - 1))),
                   pl.BlockSpec((tm, tn), ctx_idx),
                   pl.BlockSpec((tm, tn), ctx_idx),
                   pl.BlockSpec((None, tm, tn), lat_idx),
                   pl.BlockSpec((None, tm, tn), lat_idx),
                   pl.BlockSpec((tm, tn), ctx_idx),
                   pl.BlockSpec((tm, tn), ctx_idx)],
        out_shape=[jax.ShapeDtypeStruct((t, q_cols), BF16),
                   jax.ShapeDtypeStruct((geom.t_p, tn), BF16),
                   jax.ShapeDtypeStruct((geom.t_p, tn), BF16),
                   kv_lat, kv_lat,
                   jax.ShapeDtypeStruct((geom.t_p, tn), F32),
                   jax.ShapeDtypeStruct((geom.t_p, tn), F32)],
        scratch_shapes=[pltpu.VMEM((tm, d), BF16)],
        compiler_params=_params(2),
        name="attn_qkv",
    )(x, mods, gain.reshape(1, d), w, cos, sin, q_gain.reshape(1, hd), k_gain.reshape(1, hd))


def _append_cache_kernel(ck_ref, cv_ref, k_in, v_in, k_ref, v_ref):
    k_ref[...] = ck_ref[...].astype(k_ref.dtype)
    v_ref[...] = cv_ref[...].astype(v_ref.dtype)


def _append_cache(k_lat, v_lat, cache_k, cache_v, len_s):
    n_s, past, kv = cache_k.shape
    assert len_s % past == 0
    cache_spec = pl.BlockSpec((None, past, kv), lambda b: (b, 0, 0))
    tail_spec = pl.BlockSpec((None, past, kv), lambda b: (b, len_s // past, 0))
    any_spec = pl.BlockSpec(memory_space=pl.ANY)
    return pl.pallas_call(
        _append_cache_kernel,
        grid=(n_s,),
        in_specs=[cache_spec, cache_spec, any_spec, any_spec],
        out_specs=[tail_spec, tail_spec],
        out_shape=[jax.ShapeDtypeStruct(k_lat.shape, k_lat.dtype)] * 2,
        input_output_aliases={2: 0, 3: 1},
        compiler_params=_params(1),
        name="append_cache",
    )(cache_k, cache_v, k_lat, v_lat)


def _attn_kernel(*refs, rep, aliased, heads_per_chain):
    q_ref, k_ref, v_ref = refs[0], refs[1], refs[2]
    o_ref = refs[4] if aliased else refs[3]
    hd = ATTN_HEAD_DIM
    rows = q_ref.shape[0]
    k = k_ref[...]
    v = v_ref[...]
    for c0 in range(0, rep, heads_per_chain):
        heads = range(c0, c0 + heads_per_chain)
        q = jnp.concatenate([q_ref[:, r * hd:(r + 1) * hd] for r in heads], axis=0)
        s = lax.dot_general(q, k, (((1,), (1,)), ((), ())), preferred_element_type=F32)
        p = jnp.exp2(s - jnp.max(s, axis=-1, keepdims=True))
        denom = jnp.sum(p, axis=-1, keepdims=True)
        o = _dot(p.astype(BF16), v) / denom
        for n, r in enumerate(heads):
            o_ref[:, r * hd:(r + 1) * hd] = o[n * rows:(n + 1) * rows, :].astype(o_ref.dtype)


def _attention(q, k_ctx, v_ctx, k_lat, v_lat, geom, rep):
    t, qd = q.shape
    hd, n_kv = ATTN_HEAD_DIM, N_KV_HEADS
    qb = ATTN_Q_ROWS
    assert geom.len_p % qb == 0 and geom.len_s % qb == 0 and rep % ATTN_HEADS_PER_CHAIN == 0
    out_shape = jax.ShapeDtypeStruct((t, qd), BF16)
    blk_q = (qb, rep * hd)
    p_blocks = geom.len_p // qb
    o = pl.pallas_call(
        functools.partial(_attn_kernel, rep=rep, aliased=False, heads_per_chain=rep),
        grid=(geom.n_p, n_kv, p_blocks),
        in_specs=[pl.BlockSpec(blk_q, lambda b, h, i: (b * p_blocks + i, h)),
                  pl.BlockSpec((geom.len_p, hd), lambda b, h, i: (b, h)),
                  pl.BlockSpec((geom.len_p, hd), lambda b, h, i: (b, h))],
        out_specs=pl.BlockSpec(blk_q, lambda b, h, i: (b * p_blocks + i, h)),
        out_shape=out_shape,
        compiler_params=_params(3),
        name="attn_ctx",
    )(q, k_ctx, v_ctx)
    s_blocks = geom.len_s // qb
    q0 = geom.t_p // qb
    l_all = k_lat.shape[1]
    return pl.pallas_call(
        functools.partial(_attn_kernel, rep=rep, aliased=True, heads_per_chain=ATTN_HEADS_PER_CHAIN),
        grid=(geom.n_s, n_kv, s_blocks),
        in_specs=[pl.BlockSpec(blk_q, lambda b, h, i: (q0 + b * s_blocks + i, h)),
                  pl.BlockSpec((None, l_all, hd), lambda b, h, i: (b, 0, h)),
                  pl.BlockSpec((None, l_all, hd), lambda b, h, i: (b, 0, h)),
                  pl.BlockSpec(memory_space=pl.ANY)],
        out_specs=pl.BlockSpec(blk_q, lambda b, h, i: (q0 + b * s_blocks + i, h)),
        out_shape=out_shape,
        input_output_aliases={3: 0},
        compiler_params=_params(3),
        name="attn_lat",
    )(q, k_lat, v_lat, o)


def _rope_tables(length):
    half = ATTN_HEAD_DIM // 2
    pos = jnp.arange(length, dtype=jnp.int32)
    row_pos = (pos // GRID_W).astype(F32)
    col_pos = (pos % GRID_W).astype(F32)
    inv = ROPE_THETA ** (-jnp.arange(0, half, 2, dtype=F32) / half)
    ang_r = row_pos[:, None] * inv[None, :]
    ang_c = col_pos[:, None] * inv[None, :]
    cos = jnp.concatenate([jnp.cos(ang_r)] * 2 + [jnp.cos(ang_c)] * 2, axis=-1)
    sin = jnp.concatenate([-jnp.sin(ang_r), jnp.sin(ang_r), -jnp.sin(ang_c), jnp.sin(ang_c)], axis=-1)
    return cos, sin


def kernel(x_prompt, x_sample, state_ssd_fwd, state_ssd_bwd, cache_k, cache_v, c, c_ctx, w_mod, b_mod, norm_mix, norm_ffn, ssd_w_in, ssd_conv_w, ssd_conv_b, ssd_a_log, ssd_dt_bias, ssd_d, ssd_norm, ssd_w_out, attn_w_qkv, attn_q_norm, attn_k_norm, attn_w_o, ffn_w_gu, ffn_w_down, final_norm):
    n_p, len_p, d = x_prompt.shape
    n_s, len_s, _ = x_sample.shape
    depth = w_mod.shape[0]
    geom = _Geom(n_p, len_p, n_s, len_s)
    d_ff = ffn_w_down.shape[1]
    d_inner = ssd_w_out.shape[1]
    heads = d_inner // SSD_HEAD_DIM
    hpg = heads // SSD_GROUPS
    gw = hpg * SSD_HEAD_DIM
    conv_dim = d_inner + 2 * SSD_GROUPS * D_STATE
    n_heads = attn_w_o.shape[1] // ATTN_HEAD_DIM
    rep = n_heads // N_KV_HEADS

    x = (x_prompt.reshape(geom.t_p, d), x_sample.reshape(geom.t_s, d))
    w_gu_all = ffn_w_gu.astype(BF16)
    w_down_all = ffn_w_down.astype(BF16)

    cond_rows = -(-(1 + n_s) // 8) * 8
    cond = jnp.zeros((cond_rows, d), F32).at[0].set(c_ctx).at[1:1 + n_s].set(c)
    mods_all = _adaln(cond, w_mod, b_mod).reshape(depth, cond_rows, 6, d)

    new_f, new_b, new_k, new_v = [], [], [], []
    for i in range(depth):
        mods = mods_all[i]
        j = i // 2
        if i % 2 == 0:
            w_in = ssd_w_in[j].astype(BF16)
            z, xbc, dt_raw = _ssd_in_proj(geom, x, mods, norm_mix[i], w_in, d_inner, conv_dim, 2 * heads, 1024)
            xbc = _ssd_conv(geom, xbc, ssd_conv_w[j], ssd_conv_b[j])
            decays = _dt_prep(dt_raw, ssd_a_log[j], ssd_dt_bias[j])
            dsk = jnp.repeat(ssd_d[j, 0] + ssd_d[j, 1], SSD_HEAD_DIM).reshape(SSD_GROUPS, 1, gw)
            ng = ssd_norm[j].reshape(SSD_GROUPS, 1, gw)
            s0_f = state_ssd_fwd[:, j].reshape(n_s, SSD_GROUPS, gw, D_STATE)
            s0_b = state_ssd_bwd[:, j].reshape(n_s, SSD_GROUPS, gw, D_STATE)
            y_b, s_b = _ssd_scan(geom, 1, xbc, decays, s0_b)
            y, s_f = _ssd_scan(geom, 0, xbc, decays, s0_f, y_b, z, dsk, ng)
            new_f.append(s_f.reshape(n_p, heads, SSD_HEAD_DIM, D_STATE))
            new_b.append(s_b.reshape(n_p, heads, SSD_HEAD_DIM, D_STATE))
            x = _matmul_gate_res(geom, y, ssd_w_out[j].astype(BF16), 0, x, mods, 2, 512, "ssd_out")
        else:
            w_qkv = attn_w_qkv[j].astype(BF16)
            cos, sin = _rope_tables(len_s)
            kv_w = N_KV_HEADS * ATTN_HEAD_DIM
            past = cache_k.shape[2]
            q, k_c, v_c, k_lat, v_lat, k_ctx, v_ctx = _attn_qkv(
                geom, x, mods, norm_mix[i], w_qkv, cos, sin, attn_q_norm[j], attn_k_norm[j], n_heads, past)
            k_lat, v_lat = _append_cache(k_lat, v_lat, cache_k[:, j].reshape(n_s, past, kv_w),
                                         cache_v[:, j].reshape(n_s, past, kv_w), len_s)
            o = _attention(q, k_c, v_c, k_lat, v_lat, geom, rep)
            new_k.append(k_ctx.reshape(n_p, len_p, N_KV_HEADS, ATTN_HEAD_DIM))
            new_v.append(v_ctx.reshape(n_p, len_p, N_KV_HEADS, ATTN_HEAD_DIM))
            x = _matmul_gate_res(geom, o, attn_w_o[j].astype(BF16), 0, x, mods, 2, 1024, "attn_out")
        hidden = _norm_mod_matmul(geom, x, mods, norm_ffn[i], w_gu_all, i, [0, d_ff], d_ff, 512, 3, "swiglu", BF16,
                                  "ffn_up")
        x = _matmul_gate_res(geom, hidden, w_down_all, i, x, mods, 5, 512, "ffn_down")

    y_prompt = _final_norm(x, final_norm, 0, geom.t_p, "final_norm_ctx").reshape(n_p, len_p, d)
    y_sample = _final_norm(x, final_norm, geom.t_p, geom.t_s, "final_norm_lat").reshape(n_s, len_s, d)
    return (y_prompt, y_sample, jnp.stack(new_f, axis=1), jnp.stack(new_b, axis=1),
            jnp.stack(new_k, axis=1), jnp.stack(new_v, axis=1))
```

```python
import functools
import math

import jax
import jax.numpy as jnp
from jax import lax
from jax.experimental import pallas as pl
from jax.experimental.pallas import tpu as pltpu

F32 = jnp.float32
BF16 = jnp.bfloat16

EPS = 1e-6
GRID_W = 64
SSD_HEAD_DIM = 64
SSD_GROUPS = 8
D_STATE = 128
D_CONV = 5
CHUNK = 128
ATTN_HEAD_DIM = 128
N_KV_HEADS = 4
ROPE_THETA = 10000.0

VMEM_LIMIT_BYTES = 56 * 1024 * 1024
MAX_TOKEN_TILE = 1024
SCAN_ROWS = 2 * CHUNK
CONV_ROWS = 256
CONV_HALO = 16
ATTN_Q_ROWS = 256
ATTN_LAT_Q_ROWS = 512
ATTN_HEADS_PER_CHAIN = 1
SCAN_GROUPS_PER_STEP = 8
PROLOGUE_ROWS = 256
QKV_EPILOGUE_ROWS = 128
LOG2E = math.log2(math.e)


def _params(n_axes):
    return pltpu.CompilerParams(dimension_semantics=("arbitrary",) * n_axes,
                                vmem_limit_bytes=VMEM_LIMIT_BYTES)


def _silu(v):
    return v * jax.nn.sigmoid(v)


def _softplus(v):
    return jnp.maximum(v, 0.0) + jnp.log1p(jnp.exp(-jnp.abs(v)))


def _dot(a, b):
    return jnp.dot(a, b, preferred_element_type=F32)


def _split2(v):
    hi = v.astype(BF16)
    lo = (v - hi.astype(F32)).astype(BF16)
    return hi, lo


def _split3(v):
    hi = v.astype(BF16)
    r = v - hi.astype(F32)
    mid = r.astype(BF16)
    lo = (r - mid.astype(F32)).astype(BF16)
    return hi, mid, lo


def _adaln_kernel(c_ref, w_ref, b_ref, o_ref):
    s = _silu(c_ref[...]).astype(BF16)
    o_ref[0] = _dot(s, w_ref[0].astype(BF16)) + b_ref[0]


def _adaln(cond, w_mod, b_mod):
    depth, d, n = w_mod.shape
    rows = cond.shape[0]
    tn = 1536
    assert n % tn == 0
    return pl.pallas_call(
        _adaln_kernel,
        grid=(depth, n // tn),
        in_specs=[pl.BlockSpec((rows, d), lambda l, j: (0, 0)),
                  pl.BlockSpec((1, d, tn), lambda l, j: (l, 0, j)),
                  pl.BlockSpec((1, 1, tn), lambda l, j: (l, 0, j))],
        out_specs=pl.BlockSpec((1, rows, tn), lambda l, j: (l, 0, j)),
        out_shape=jax.ShapeDtypeStruct((depth, rows, n), F32),
        compiler_params=_params(2),
        name="adaln",
    )(cond, w_mod, b_mod.reshape(depth, 1, n))


class _Geom:
    def __init__(self, n_p, len_p, n_s, len_s):
        self.n_p, self.len_p, self.n_s, self.len_s = n_p, len_p, n_s, len_s
        self.t_p = n_p * len_p
        self.t_s = n_s * len_s
        self.t = self.t_p + self.t_s
        self.tm = math.gcd(math.gcd(self.t_p, len_s), MAX_TOKEN_TILE)

    def mod_row(self, i, tile):
        npt = self.t_p // tile
        per_seq = self.len_s // tile
        return jnp.where(i < npt, 0, 1 + (i - npt) // per_seq)


def _row_blocks(n_rows, block):
    return [slice(r0, r0 + block) for r0 in range(0, n_rows, block)]


def _norm_mod_rows(x_ref, mod_ref, g_ref, rows, shift_row):
    x = x_ref[rows, :]
    y = x * lax.rsqrt(jnp.mean(x * x, axis=-1, keepdims=True) + EPS) * g_ref[...]
    scale1 = 1.0 + mod_ref[0, shift_row + 1:shift_row + 2, :]
    return (y * scale1 + mod_ref[0, shift_row:shift_row + 1, :]).astype(BF16)


def _ssd_in_kernel(mod_ref, g_ref, w_ref, wdt_ref, *rest, z_blocks, sub_rows, n_p_tiles):
    x_refs = rest[:-4]
    z_ref, xbc_ref, dt_ref, h_ref = rest[-4:]
    j = pl.program_id(1)
    first = j == 0

    def first_step(x_ref):
        for rows in _row_blocks(z_ref.shape[0], sub_rows):
            h = _norm_mod_rows(x_ref, mod_ref, g_ref, rows, 0)
            h_ref[rows, :] = h
            z_ref[rows, :] = _dot(h, w_ref[...]).astype(z_ref.dtype)

    if len(x_refs) == 1:
        pl.when(first)(lambda: first_step(x_refs[0]))
    else:
        is_p = pl.program_id(0) < n_p_tiles
        pl.when(jnp.logical_and(first, is_p))(lambda: first_step(x_refs[0]))
        pl.when(jnp.logical_and(first, jnp.logical_not(is_p)))(lambda: first_step(x_refs[1]))

    @pl.when(jnp.logical_not(first))
    def _():
        h = h_ref[...]
        acc = _dot(h, w_ref[...])

        @pl.when(j < z_blocks)
        def _():
            z_ref[...] = acc.astype(z_ref.dtype)

        @pl.when(j >= z_blocks)
        def _():
            xbc_ref[...] = acc.astype(xbc_ref.dtype)

        @pl.when(j == pl.num_programs(1) - 1)
        def _():
            dt_ref[...] = _dot(h, wdt_ref[...])


def _ssd_in_proj(geom, x, mods, gain, w, d_inner, conv_dim, n_dt, tn):
    t, d = geom.t, w.shape[0]
    tm = geom.tm
    assert d_inner % tn == 0 and conv_dim % tn == 0 and (d_inner + conv_dim) % n_dt == 0
    z_blocks, xbc_blocks = d_inner // tn, conv_dim // tn
    dt_block = (d_inner + conv_dim) // n_dt
    xs, x_specs = _residual_specs(geom, x, (tm, d), lambda j: 0, single_buffer=True)
    return pl.pallas_call(
        functools.partial(_ssd_in_kernel, z_blocks=z_blocks, sub_rows=min(tm, PROLOGUE_ROWS),
                          n_p_tiles=geom.t_p // tm),
        grid=(t // tm, z_blocks + xbc_blocks),
        in_specs=[pl.BlockSpec((1, 6, d), lambda i, j: (geom.mod_row(i, tm), 0, 0)),
                  pl.BlockSpec((1, d), lambda i, j: (0, 0)),
                  pl.BlockSpec((d, tn), lambda i, j: (0, j)),
                  pl.BlockSpec((d, n_dt), lambda i, j: (0, dt_block))] + x_specs,
        out_specs=[pl.BlockSpec((tm, tn), lambda i, j: (i, jnp.minimum(j, z_blocks - 1))),
                   pl.BlockSpec((tm, tn), lambda i, j: (i, jnp.maximum(j - z_blocks, 0))),
                   pl.BlockSpec((tm, n_dt), lambda i, j: (i, 0))],
        out_shape=[jax.ShapeDtypeStruct((t, d_inner), BF16),
                   jax.ShapeDtypeStruct((t, conv_dim), BF16),
                   jax.ShapeDtypeStruct((t, n_dt), F32)],
        scratch_shapes=[pltpu.VMEM((tm, d), BF16)],
        compiler_params=_params(2),
        name="ssd_in",
    )(mods, gain.reshape(1, d), w, w, *xs)


def _weight_spec(w, layer, k, tn, col_block0):
    if w.ndim == 2:
        return pl.BlockSpec((k, tn), lambda i, j: (0, col_block0 + j))
    return pl.BlockSpec((None, k, tn), lambda i, j: (layer, 0, col_block0 + j))


def _nm_kernel(x_ref, mod_ref, g_ref, *rest, n_w, shift_row, act, sub_rows):
    w_refs = rest[:n_w]
    o_ref = rest[n_w]
    h_ref = rest[n_w + 1]

    def emit(rows, h):
        if act == "swiglu":
            gate = _dot(h, w_refs[0][...])
            up = _dot(h, w_refs[1][...])
            o_ref[rows, :] = (_silu(gate) * up).astype(o_ref.dtype)
        else:
            o_ref[rows, :] = _dot(h, w_refs[0][...]).astype(o_ref.dtype)

    first = pl.program_id(1) == 0

    @pl.when(first)
    def _():
        for rows in _row_blocks(o_ref.shape[0], sub_rows):
            h = _norm_mod_rows(x_ref, mod_ref, g_ref, rows, shift_row)
            h_ref[rows, :] = h
            emit(rows, h)

    @pl.when(jnp.logical_not(first))
    def _():
        emit(slice(None), h_ref[...])


def _norm_mod_matmul(geom, x, mods, gain, w, layer, col_offsets, n_cols, tn, shift_row, act, out_dtype, name):
    t, d = x.shape
    tm = geom.tm
    assert n_cols % tn == 0 and all(off % tn == 0 for off in col_offsets)
    w_specs = [_weight_spec(w, layer, d, tn, off // tn) for off in col_offsets]
    return pl.pallas_call(
        functools.partial(_nm_kernel, n_w=len(col_offsets), shift_row=shift_row, act=act,
                          sub_rows=min(tm, PROLOGUE_ROWS)),
        grid=(t // tm, n_cols // tn),
        in_specs=[pl.BlockSpec((tm, d), lambda i, j: (i, 0)),
                  pl.BlockSpec((1, 6, d), lambda i, j: (geom.mod_row(i, tm), 0, 0)),
                  pl.BlockSpec((1, d), lambda i, j: (0, 0))] + w_specs,
        out_specs=pl.BlockSpec((tm, tn), lambda i, j: (i, j)),
        out_shape=jax.ShapeDtypeStruct((t, n_cols), out_dtype),
        scratch_shapes=[pltpu.VMEM((tm, d), BF16)],
        compiler_params=_params(2),
        name=name,
    )(x, mods, gain.reshape(1, d), *([w] * len(col_offsets)))


def _gate_res_kernel(a_ref, w_ref, mod_ref, *rest, gate_row, n_p_tiles):
    x_refs, o_ref = rest[:-1], rest[-1]
    gated = mod_ref[0, gate_row:gate_row + 1, :] * _dot(a_ref[...], w_ref[...])
    if len(x_refs) == 1:
        o_ref[...] = x_refs[0][...] + gated
    else:
        is_p = pl.program_id(0) < n_p_tiles

        @pl.when(is_p)
        def _():
            o_ref[...] = x_refs[0][...] + gated

        @pl.when(jnp.logical_not(is_p))
        def _():
            o_ref[...] = x_refs[1][...] + gated


def _residual_specs(geom, x, block, col_of, single_buffer=False):
    tm = geom.tm
    npt = geom.t_p // tm
    if not isinstance(x, tuple):
        return [x], [pl.BlockSpec(block, lambda i, j: (i, col_of(j)))]
    mode = dict(pipeline_mode=pl.Buffered(1)) if single_buffer else {}
    return list(x), [pl.BlockSpec(block, lambda i, j: (jnp.minimum(i, npt - 1), col_of(j)), **mode),
                     pl.BlockSpec(block, lambda i, j: (jnp.maximum(i - npt, 0), col_of(j)))]


def _matmul_gate_res(geom, a, w, layer, x, mods, gate_row, tn, name):
    t, k = a.shape
    d = w.shape[-1]
    tm = geom.tm
    assert d % tn == 0
    xs, x_specs = _residual_specs(geom, x, (tm, tn), lambda j: j)
    return pl.pallas_call(
        functools.partial(_gate_res_kernel, gate_row=gate_row, n_p_tiles=geom.t_p // tm),
        grid=(t // tm, d // tn),
        in_specs=[pl.BlockSpec((tm, k), lambda i, j: (i, 0)),
                  _weight_spec(w, layer, k, tn, 0),
                  pl.BlockSpec((1, 6, tn), lambda i, j: (geom.mod_row(i, tm), 0, j))] + x_specs,
        out_specs=pl.BlockSpec((tm, tn), lambda i, j: (i, j)),
        out_shape=jax.ShapeDtypeStruct((t, d), F32),
        compiler_params=_params(2),
        name=name,
    )(a, w, mods, *xs)


def _rmsnorm_kernel(x_ref, g_ref, o_ref):
    x = x_ref[...]
    ms = jnp.mean(x * x, axis=-1, keepdims=True)
    o_ref[...] = x * lax.rsqrt(ms + EPS) * g_ref[...]


def _final_norm(x, gain, row0, n_rows, name):
    d = x.shape[1]
    tr = 256
    assert row0 % tr == 0 and n_rows % tr == 0
    return pl.pallas_call(
        _rmsnorm_kernel,
        grid=(n_rows // tr,),
        in_specs=[pl.BlockSpec((tr, d), lambda i: (row0 // tr + i, 0)),
                  pl.BlockSpec((1, d), lambda i: (0, 0))],
        out_specs=pl.BlockSpec((tr, d), lambda i: (i, 0)),
        out_shape=jax.ShapeDtypeStruct((n_rows, d), F32),
        compiler_params=_params(1),
        name=name,
    )(x, gain.reshape(1, d))


def _conv_kernel(u_ref, prev_ref, next_ref, w_ref, b_ref, o_ref, ext_ref, *, n_p_blocks, p_blocks, s_blocks):
    i = pl.program_id(0)
    lb = u_ref.shape[0]
    is_p = i < n_p_blocks
    pos = jnp.where(is_p, i % p_blocks, (i - n_p_blocks) % s_blocks)
    last = jnp.where(is_p, p_blocks, s_blocks) - 1
    ext_ref[0:CONV_HALO, :] = jnp.where(pos > 0, prev_ref[...].astype(F32), 0.0)
    ext_ref[CONV_HALO:CONV_HALO + lb, :] = u_ref[...].astype(F32)
    ext_ref[CONV_HALO + lb:2 * CONV_HALO + lb, :] = jnp.where(pos < last, next_ref[...].astype(F32), 0.0)
    ext = ext_ref[...]
    n_ext = ext.shape[0]
    centre = D_CONV // 2
    acc = b_ref[...] + w_ref[centre:centre + 1, :] * ext[CONV_HALO:CONV_HALO + lb, :]
    for k in range(D_CONV):
        if k != centre:
            shifted = pltpu.roll(ext, (centre - k) % n_ext, 0)
            acc = acc + w_ref[k:k + 1, :] * shifted[CONV_HALO:CONV_HALO + lb, :]
    o_ref[...] = _silu(acc).astype(o_ref.dtype)


def _ssd_conv(geom, xbc, conv_w, conv_b):
    t, c = xbc.shape
    lb, ct = CONV_ROWS, 2048
    assert geom.len_p % lb == 0 and geom.len_s % lb == 0 and c % ct == 0
    hb = lb // CONV_HALO
    n_halo_blocks = t // CONV_HALO
    kern = functools.partial(_conv_kernel, n_p_blocks=geom.t_p // lb, p_blocks=geom.len_p // lb,
                             s_blocks=geom.len_s // lb)
    return pl.pallas_call(
        kern,
        grid=(t // lb, c // ct),
        in_specs=[pl.BlockSpec((lb, ct), lambda i, j: (i, j)),
                  pl.BlockSpec((CONV_HALO, ct), lambda i, j: (jnp.maximum(i * hb - 1, 0), j)),
                  pl.BlockSpec((CONV_HALO, ct), lambda i, j: (jnp.minimum((i + 1) * hb, n_halo_blocks - 1), j)),
                  pl.BlockSpec((D_CONV, ct), lambda i, j: (0, j)),
                  pl.BlockSpec((1, ct), lambda i, j: (0, j))],
        out_specs=pl.BlockSpec((lb, ct), lambda i, j: (i, j)),
        out_shape=jax.ShapeDtypeStruct((t, c), BF16),
        scratch_shapes=[pltpu.VMEM((lb + 2 * CONV_HALO, ct), F32)],
        compiler_params=_params(2),
        name="ssd_conv",
    )(xbc, xbc, xbc, conv_w, conv_b.reshape(1, c))


def _dt_prep_kernel(dt_ref, alog_ref, bias_ref, cs_ref, src_ref, e_ref, w_ref):
    q = CHUNK
    n_dir_heads = dt_ref.shape[1]
    li = lax.broadcasted_iota(jnp.int32, (q, q), 0)
    si = lax.broadcasted_iota(jnp.int32, (q, q), 1)
    lower = (li >= si).astype(BF16)
    upper = (li <= si).astype(BF16)
    fwd = lax.broadcasted_iota(jnp.int32, (1, n_dir_heads), 1) < n_dir_heads // 2
    a_scale = -jnp.exp(alog_ref[...]) * LOG2E
    for r0 in range(0, dt_ref.shape[0], q):
        dt = _softplus(dt_ref[r0:r0 + q, :] + bias_ref[...])
        parts = _split3(dt * a_scale)
        cs = jnp.where(fwd, sum(_dot(lower, p) for p in parts), sum(_dot(upper, p) for p in parts))
        total = jnp.where(fwd, cs[q - 1:q, :], cs[0:1, :])
        cs_ref[:, r0:r0 + q] = cs.T
        src_ref[:, r0:r0 + q] = (cs - jnp.log2(dt)).T
        e_ref[:, r0:r0 + q] = jnp.exp2(cs).T
        w_ref[:, r0:r0 + q] = (dt * jnp.exp2(total - cs)).T


def _dt_prep(dt_raw, a_log, dt_bias):
    t, nh = dt_raw.shape
    rb = 1024 if t % 1024 == 0 else SCAN_ROWS
    out = jax.ShapeDtypeStruct((nh, t), F32)
    head_major = pl.BlockSpec((nh, rb), lambda i: (0, i))
    return pl.pallas_call(
        _dt_prep_kernel,
        grid=(t // rb,),
        in_specs=[pl.BlockSpec((rb, nh), lambda i: (i, 0)),
                  pl.BlockSpec((1, nh), lambda i: (0, 0)),
                  pl.BlockSpec((1, nh), lambda i: (0, 0))],
        out_specs=[head_major] * 4,
        out_shape=[out] * 4,
        compiler_params=_params(1),
        name="ssd_dt_prep",
    )(dt_raw, a_log.reshape(1, nh), dt_bias.reshape(1, nh))


def _scan_kernel(*refs, rev, fuse_norm, n_steps, n_p_steps, p_steps, s_steps, hpg, gps):
    if fuse_norm:
        (x_ref, b_ref, c_ref, cs_ref, src_ref, e_ref, w_ref, s0_ref, yo_ref, z_ref, dsk_ref, ng_ref,
         y_ref, sfin_ref, st_ref) = refs
    else:
        (x_ref, b_ref, c_ref, cs_ref, src_ref, e_ref, w_ref, s0_ref,
         y_ref, sfin_ref, st_ref) = refs
    q = CHUNK
    hd = SSD_HEAD_DIM
    n = D_STATE
    width = hpg * hd
    step = pl.program_id(1)
    blk = (n_steps - 1 - step) if rev else step
    is_p = blk < n_p_steps
    pos = jnp.where(is_p, blk % p_steps, (blk - n_p_steps) % s_steps)
    last = jnp.where(is_p, p_steps, s_steps) - 1
    seq_start = (pos == last) if rev else (pos == 0)
    seq_end = (pos == 0) if rev else (pos == last)

    @pl.when(jnp.logical_and(seq_start, is_p))
    def _():
        st_ref[...] = jnp.zeros_like(st_ref)

    @pl.when(jnp.logical_and(seq_start, jnp.logical_not(is_p)))
    def _():
        for gi in range(gps):
            st_ref[gi] = s0_ref[gi].T

    li = lax.broadcasted_iota(jnp.int32, (q, q), 0)
    si = lax.broadcasted_iota(jnp.int32, (q, q), 1)
    keep = (li <= si) if rev else (li >= si)
    expand = (lax.broadcasted_iota(jnp.int32, (2 * hpg, width), 1) // hd
              == lax.broadcasted_iota(jnp.int32, (2 * hpg, width), 0) % hpg).astype(BF16)
    lane = lax.broadcasted_iota(jnp.int32, (q, 2 * hd), 1)
    edge = 0 if rev else q - 1

    def widen(v_r):
        hi, lo = _split2(v_r)
        return lax.dot_general(jnp.concatenate([hi, lo], axis=0), expand, (((0,), (0,)), ((), ())),
                               preferred_element_type=F32)

    def chunk(gi, r0):
        cols = slice(gi * width, (gi + 1) * width)
        heads = slice(gi * hpg, (gi + 1) * hpg)
        x = x_ref[r0:r0 + q, cols]
        bm = b_ref[r0:r0 + q, gi * n:(gi + 1) * n]
        cm = c_ref[r0:r0 + q, gi * n:(gi + 1) * n]
        cs_c = cs_ref[heads, r0:r0 + q].T
        src_r = src_ref[heads, r0:r0 + q]
        e_x = widen(e_ref[heads, r0:r0 + q])
        w_x = widen(w_ref[heads, r0:r0 + q])

        cb = lax.dot_general(cm, bm, (((1,), (1,)), ((), ())), preferred_element_type=F32)
        state = st_ref[gi]
        y = _dot(cm, state.astype(BF16)) * e_x
        parts = []
        for p in range(hpg // 2):
            ms = []
            for r in (2 * p, 2 * p + 1):
                seg = cs_c[:, r:r + 1] - src_r[r:r + 1, :]
                ms.append((cb * jnp.exp2(jnp.where(keep, seg, -jnp.inf))).astype(BF16))
            x2 = x[:, 2 * p * hd:2 * (p + 1) * hd]
            zero = jnp.zeros_like(x2)
            rhs = jnp.concatenate([jnp.where(lane < hd, x2, zero), jnp.where(lane >= hd, x2, zero)], axis=0)
            parts.append(_dot(jnp.concatenate(ms, axis=1), rhs))
        y = y + jnp.concatenate(parts, axis=1)

        xf = x.astype(F32)
        xw = (xf * w_x).astype(BF16)
        st_ref[gi] = state * e_x[edge:edge + 1, :] + lax.dot_general(
            bm, xw, (((0,), (0,)), ((), ())), preferred_element_type=F32)

        if fuse_norm:
            y = y + yo_ref[r0:r0 + q, cols].astype(F32) + dsk_ref[gi] * xf
            yz = y * _silu(z_ref[r0:r0 + q, cols].astype(F32))
            yz = yz * lax.rsqrt(jnp.mean(yz * yz, axis=-1, keepdims=True) + EPS)
            y = yz * ng_ref[gi]
        y_ref[r0:r0 + q, cols] = y.astype(y_ref.dtype)

    n_chunks = x_ref.shape[0] // q
    for ci in (range(n_chunks - 1, -1, -1) if rev else range(n_chunks)):
        for gi in range(gps):
            chunk(gi, ci * q)

    @pl.when(jnp.logical_and(seq_end, is_p))
    def _():
        for gi in range(gps):
            sfin_ref[gi] = st_ref[gi].T


def _ssd_scan(geom, direction, xbc, decays, s0, y_other=None, z=None, dsk=None, ng=None):
    t = xbc.shape[0]
    g_n = SSD_GROUPS
    gps = SCAN_GROUPS_PER_STEP
    hpg = decays[0].shape[0] // (2 * g_n)
    width = hpg * SSD_HEAD_DIM
    n = D_STATE
    rb = SCAN_ROWS
    assert geom.len_p % rb == 0 and geom.len_s % rb == 0 and g_n % gps == 0
    n_steps = t // rb
    n_p_steps, p_steps, s_steps = geom.t_p // rb, geom.len_p // rb, geom.len_s // rb
    rev = direction == 1
    fuse = y_other is not None
    x_cols = g_n * width
    b_blk0 = x_cols // (gps * n)
    c_blk0 = (x_cols + g_n * n) // (gps * n)
    head_blk0 = direction * (g_n // gps)

    def blk(s):
        return (n_steps - 1 - s) if rev else s

    def s0_idx(g, s):
        return (jnp.clip((blk(s) - n_p_steps) // s_steps, 0, geom.n_s - 1), g, 0, 0)

    def sfin_idx(g, s):
        return (jnp.minimum(blk(s) // p_steps, geom.n_p - 1), g, 0, 0)

    wide = pl.BlockSpec((rb, gps * width), lambda g, s: (blk(s), g))
    head_major = pl.BlockSpec((gps * hpg, rb), lambda g, s: (head_blk0 + g, blk(s)))
    in_specs = [wide,
                pl.BlockSpec((rb, gps * n), lambda g, s: (blk(s), b_blk0 + g)),
                pl.BlockSpec((rb, gps * n), lambda g, s: (blk(s), c_blk0 + g)),
                head_major, head_major, head_major, head_major,
                pl.BlockSpec((None, gps, width, n), s0_idx)]
    args = [xbc, xbc, xbc, *decays, s0]
    if fuse:
        in_specs += [wide, wide,
                     pl.BlockSpec((gps, 1, width), lambda g, s: (g, 0, 0)),
                     pl.BlockSpec((gps, 1, width), lambda g, s: (g, 0, 0))]
        args += [y_other, z, dsk, ng]
    kern = functools.partial(_scan_kernel, rev=rev, fuse_norm=fuse, n_steps=n_steps, n_p_steps=n_p_steps,
                             p_steps=p_steps, s_steps=s_steps, hpg=hpg, gps=gps)
    return pl.pallas_call(
        kern,
        grid=(g_n // gps, n_steps),
        in_specs=in_specs,
        out_specs=[wide, pl.BlockSpec((None, gps, width, n), sfin_idx)],
        out_shape=[jax.ShapeDtypeStruct((t, x_cols), BF16),
                   jax.ShapeDtypeStruct((geom.n_p, g_n, width, n), F32)],
        scratch_shapes=[pltpu.VMEM((gps, n, width), F32)],
        compiler_params=_params(2),
        name="ssd_scan_bwd" if rev else "ssd_scan_fwd",
    )(*args)


def _attn_qkv_kernel(x_ref, mod_ref, g_ref, w_ref, cos_ref, sin_ref, qg_ref, kg_ref,
                     q_ref, kc_ref, vc_ref, kl_ref, vl_ref, kf_ref, vf_ref, h_ref,
                     *, n_p_tiles, q_blocks, sub_rows):
    i = pl.program_id(0)
    j = pl.program_id(1)
    hd = ATTN_HEAD_DIM
    is_p = i < n_p_tiles
    is_s = jnp.logical_not(is_p)
    first = j == 0
    tm, tn = q_ref.shape
    heads = tn // hd
    blocks = _row_blocks(tm, sub_rows)
    lane = lax.broadcasted_iota(jnp.int32, (sub_rows, hd), 1)
    first_half = (lane % (hd // 2)) < (hd // 4)

    def norm(v, gain):
        return v * lax.rsqrt(jnp.mean(v * v, axis=-1, keepdims=True) + EPS) * gain

    def rope(v, c, s):
        partner = jnp.where(first_half, pltpu.roll(v, hd - hd // 4, 1), pltpu.roll(v, hd // 4, 1))
        return v * c + partner * s

    def q_rows(rows, h):
        q_scale = hd ** -0.5 * LOG2E
        acc = _dot(h, w_ref[...])
        cos_q = jnp.where(is_p, 1.0, cos_ref[rows, :]) * q_scale
        sin_q = jnp.where(is_p, 0.0, sin_ref[rows, :]) * q_scale
        for hh in range(heads):
            cols = slice(hh * hd, (hh + 1) * hd)
            q_ref[rows, cols] = rope(norm(acc[:, cols], qg_ref[...]), cos_q, sin_q).astype(q_ref.dtype)

    @pl.when(first)
    def _():
        for rows in blocks:
            h = _norm_mod_rows(x_ref, mod_ref, g_ref, rows, 0)
            h_ref[rows, :] = h
            q_rows(rows, h)

    @pl.when(jnp.logical_and(jnp.logical_not(first), j < q_blocks))
    def _():
        for rows in blocks:
            q_rows(rows, h_ref[rows, :])

    @pl.when(jnp.logical_and(j == q_blocks, is_p))
    def _():
        for rows in blocks:
            acc = _dot(h_ref[rows, :], w_ref[...])
            for hh in range(heads):
                cols = slice(hh * hd, (hh + 1) * hd)
                normed = norm(acc[:, cols], kg_ref[...])
                kf_ref[rows, cols] = normed
                kc_ref[rows, cols] = normed.astype(kc_ref.dtype)

    @pl.when(jnp.logical_and(j == q_blocks, is_s))
    def _():
        for rows in blocks:
            acc = _dot(h_ref[rows, :], w_ref[...])
            for hh in range(heads):
                cols = slice(hh * hd, (hh + 1) * hd)
                normed = norm(acc[:, cols], kg_ref[...])
                kl_ref[rows, cols] = rope(normed, cos_ref[rows, :], sin_ref[rows, :]).astype(kl_ref.dtype)

    @pl.when(jnp.logical_and(j == q_blocks + 1, is_p))
    def _():
        acc = _dot(h_ref[...], w_ref[...])
        vf_ref[...] = acc
        vc_ref[...] = acc.astype(vc_ref.dtype)

    @pl.when(jnp.logical_and(j == q_blocks + 1, is_s))
    def _():
        vl_ref[...] = _dot(h_ref[...], w_ref[...]).astype(vl_ref.dtype)


def _attn_qkv(geom, x, mods, gain, w, cos, sin, q_gain, k_gain, n_heads, past):
    t, d = x.shape
    tm = geom.tm
    hd, n_kv = ATTN_HEAD_DIM, N_KV_HEADS
    tn = n_kv * hd
    q_cols = n_heads * hd
    assert q_cols % tn == 0 and q_cols >= tn and w.shape[1] == q_cols + 2 * tn
    q_blocks = q_cols // tn
    npt = geom.t_p // tm
    per_seq = geom.len_s // tm

    def tab_idx(i, j):
        return (jnp.where(i < npt, 0, (i - npt) % per_seq), 0)

    def ctx_idx(i, j):
        return (jnp.minimum(i, npt - 1), 0)

    def lat_idx(i, j):
        return (jnp.clip((i - npt) // per_seq, 0, geom.n_s - 1), jnp.where(i < npt, 0, (i - npt) % per_seq), 0)

    kv_lat = jax.ShapeDtypeStruct((geom.n_s, geom.len_s + past, tn), BF16)
    return pl.pallas_call(
        functools.partial(_attn_qkv_kernel, n_p_tiles=npt, q_blocks=q_blocks,
                          sub_rows=min(tm, QKV_EPILOGUE_ROWS)),
        grid=(t // tm, q_blocks + 2),
        in_specs=[pl.BlockSpec((tm, d), lambda i, j: (i, 0)),
                  pl.BlockSpec((1, 6, d), lambda i, j: (geom.mod_row(i, tm), 0, 0)),
                  pl.BlockSpec((1, d), lambda i, j: (0, 0)),
                  pl.BlockSpec((d, tn), lambda i, j: (0, j)),
                  pl.BlockSpec((tm, hd), tab_idx),
                  pl.BlockSpec((tm, hd), tab_idx),
                  pl.BlockSpec((1, hd), lambda i, j: (0, 0)),
                  pl.BlockSpec((1, hd), lambda i, j: (0, 0))],
        out_specs=[pl.BlockSpec((tm, tn), lambda i, j: (i, jnp.minimum(j, q_blocks - 1))),
                   pl.BlockSpec((tm, tn), ctx_idx),
                   pl.BlockSpec((tm, tn), ctx_idx),
                   pl.BlockSpec((None, tm, tn), lat_idx),
                   pl.BlockSpec((None, tm, tn), lat_idx),
                   pl.BlockSpec((tm, tn), ctx_idx),
                   pl.BlockSpec((tm, tn), ctx_idx)],
        out_shape=[jax.ShapeDtypeStruct((t, q_cols), BF16),
                   jax.ShapeDtypeStruct((geom.t_p, tn), BF16),
                   jax.ShapeDtypeStruct((geom.t_p, tn), BF16),
                   kv_lat, kv_lat,
                   jax.ShapeDtypeStruct((geom.t_p, tn), F32),
                   jax.ShapeDtypeStruct((geom.t_p, tn), F32)],
        scratch_shapes=[pltpu.VMEM((tm, d), BF16)],
        compiler_params=_params(2),
        name="attn_qkv",
    )(x, mods, gain.reshape(1, d), w, cos, sin, q_gain.reshape(1, hd), k_gain.reshape(1, hd))


def _append_cache_kernel(ck_ref, cv_ref, k_in, v_in, k_ref, v_ref):
    k_ref[...] = ck_ref[...].astype(k_ref.dtype)
    v_ref[...] = cv_ref[...].astype(v_ref.dtype)


def _append_cache(k_lat, v_lat, cache_k, cache_v, len_s):
    n_s, past, kv = cache_k.shape
    assert len_s % past == 0
    cache_spec = pl.BlockSpec((None, past, kv), lambda b: (b, 0, 0))
    tail_spec = pl.BlockSpec((None, past, kv), lambda b: (b, len_s // past, 0))
    any_spec = pl.BlockSpec(memory_space=pl.ANY)
    return pl.pallas_call(
        _append_cache_kernel,
        grid=(n_s,),
        in_specs=[cache_spec, cache_spec, any_spec, any_spec],
        out_specs=[tail_spec, tail_spec],
        out_shape=[jax.ShapeDtypeStruct(k_lat.shape, k_lat.dtype)] * 2,
        input_output_aliases={2: 0, 3: 1},
        compiler_params=_params(1),
        name="append_cache",
    )(cache_k, cache_v, k_lat, v_lat)


def _attn_kernel(*refs, rep, aliased, heads_per_chain):
    q_ref, k_ref, v_ref = refs[0], refs[1], refs[2]
    o_ref = refs[4] if aliased else refs[3]
    hd = ATTN_HEAD_DIM
    rows = q_ref.shape[0]
    k = k_ref[...]
    v = v_ref[...]
    for c0 in range(0, rep, heads_per_chain):
        heads = range(c0, c0 + heads_per_chain)
        q = jnp.concatenate([q_ref[:, r * hd:(r + 1) * hd] for r in heads], axis=0)
        s = lax.dot_general(q, k, (((1,), (1,)), ((), ())), preferred_element_type=F32)
        p = jnp.exp2(s - jnp.max(s, axis=-1, keepdims=True))
        denom = jnp.sum(p, axis=-1, keepdims=True)
        o = _dot(p.astype(BF16), v) / denom
        for n, r in enumerate(heads):
            o_ref[:, r * hd:(r + 1) * hd] = o[n * rows:(n + 1) * rows, :].astype(o_ref.dtype)


def _attention(q, k_ctx, v_ctx, k_lat, v_lat, geom, rep):
    t, qd = q.shape
    hd, n_kv = ATTN_HEAD_DIM, N_KV_HEADS
    qb = ATTN_Q_ROWS
    assert geom.len_p % qb == 0 and geom.len_s % qb == 0 and rep % ATTN_HEADS_PER_CHAIN == 0
    out_shape = jax.ShapeDtypeStruct((t, qd), BF16)
    blk_q = (qb, rep * hd)
    p_blocks = geom.len_p // qb
    o = pl.pallas_call(
        functools.partial(_attn_kernel, rep=rep, aliased=False, heads_per_chain=rep),
        grid=(geom.n_p, n_kv, p_blocks),
        in_specs=[pl.BlockSpec(blk_q, lambda b, h, i: (b * p_blocks + i, h)),
                  pl.BlockSpec((geom.len_p, hd), lambda b, h, i: (b, h)),
                  pl.BlockSpec((geom.len_p, hd), lambda b, h, i: (b, h))],
        out_specs=pl.BlockSpec(blk_q, lambda b, h, i: (b * p_blocks + i, h)),
        out_shape=out_shape,
        compiler_params=_params(3),
        name="attn_ctx",
    )(q, k_ctx, v_ctx)
    qb = math.gcd(math.gcd(geom.len_s, geom.t_p), ATTN_LAT_Q_ROWS)
    blk_q = (qb, rep * hd)
    s_blocks = geom.len_s // qb
    q0 = geom.t_p // qb
    l_all = k_lat.shape[1]
    return pl.pallas_call(
        functools.partial(_attn_kernel, rep=rep, aliased=True, heads_per_chain=ATTN_HEADS_PER_CHAIN),
        grid=(geom.n_s, n_kv, s_blocks),
        in_specs=[pl.BlockSpec(blk_q, lambda b, h, i: (q0 + b * s_blocks + i, h)),
                  pl.BlockSpec((None, l_all, hd), lambda b, h, i: (b, 0, h)),
                  pl.BlockSpec((None, l_all, hd), lambda b, h, i: (b, 0, h)),
                  pl.BlockSpec(memory_space=pl.ANY)],
        out_specs=pl.BlockSpec(blk_q, lambda b, h, i: (q0 + b * s_blocks + i, h)),
        out_shape=out_shape,
        input_output_aliases={3: 0},
        compiler_params=_params(3),
        name="attn_lat",
    )(q, k_lat, v_lat, o)


def _rope_tables(length):
    half = ATTN_HEAD_DIM // 2
    pos = jnp.arange(length, dtype=jnp.int32)
    row_pos = (pos // GRID_W).astype(F32)
    col_pos = (pos % GRID_W).astype(F32)
    inv = ROPE_THETA ** (-jnp.arange(0, half, 2, dtype=F32) / half)
    ang_r = row_pos[:, None] * inv[None, :]
    ang_c = col_pos[:, None] * inv[None, :]
    cos = jnp.concatenate([jnp.cos(ang_r)] * 2 + [jnp.cos(ang_c)] * 2, axis=-1)
    sin = jnp.concatenate([-jnp.sin(ang_r), jnp.sin(ang_r), -jnp.sin(ang_c), jnp.sin(ang_c)], axis=-1)
    return cos, sin


def kernel(x_prompt, x_sample, state_ssd_fwd, state_ssd_bwd, cache_k, cache_v, c, c_ctx, w_mod, b_mod, norm_mix, norm_ffn, ssd_w_in, ssd_conv_w, ssd_conv_b, ssd_a_log, ssd_dt_bias, ssd_d, ssd_norm, ssd_w_out, attn_w_qkv, attn_q_norm, attn_k_norm, attn_w_o, ffn_w_gu, ffn_w_down, final_norm):
    n_p, len_p, d = x_prompt.shape
    n_s, len_s, _ = x_sample.shape
    depth = w_mod.shape[0]
    geom = _Geom(n_p, len_p, n_s, len_s)
    d_ff = ffn_w_down.shape[1]
    d_inner = ssd_w_out.shape[1]
    heads = d_inner // SSD_HEAD_DIM
    hpg = heads // SSD_GROUPS
    gw = hpg * SSD_HEAD_DIM
    conv_dim = d_inner + 2 * SSD_GROUPS * D_STATE
    n_heads = attn_w_o.shape[1] // ATTN_HEAD_DIM
    rep = n_heads // N_KV_HEADS

    x = (x_prompt.reshape(geom.t_p, d), x_sample.reshape(geom.t_s, d))
    w_gu_all = ffn_w_gu.astype(BF16)
    w_down_all = ffn_w_down.astype(BF16)

    cond_rows = -(-(1 + n_s) // 8) * 8
    cond = jnp.zeros((cond_rows, d), F32).at[0].set(c_ctx).at[1:1 + n_s].set(c)
    mods_all = _adaln(cond, w_mod, b_mod).reshape(depth, cond_rows, 6, d)

    new_f, new_b, new_k, new_v = [], [], [], []
    for i in range(depth):
        mods = mods_all[i]
        j = i // 2
        if i % 2 == 0:
            w_in = ssd_w_in[j].astype(BF16)
            z, xbc, dt_raw = _ssd_in_proj(geom, x, mods, norm_mix[i], w_in, d_inner, conv_dim, 2 * heads, 1024)
            xbc = _ssd_conv(geom, xbc, ssd_conv_w[j], ssd_conv_b[j])
            decays = _dt_prep(dt_raw, ssd_a_log[j], ssd_dt_bias[j])
            dsk = jnp.repeat(ssd_d[j, 0] + ssd_d[j, 1], SSD_HEAD_DIM).reshape(SSD_GROUPS, 1, gw)
            ng = ssd_norm[j].reshape(SSD_GROUPS, 1, gw)
            s0_f = state_ssd_fwd[:, j].reshape(n_s, SSD_GROUPS, gw, D_STATE)
            s0_b = state_ssd_bwd[:, j].reshape(n_s, SSD_GROUPS, gw, D_STATE)
            y_b, s_b = _ssd_scan(geom, 1, xbc, decays, s0_b)
            y, s_f = _ssd_scan(geom, 0, xbc, decays, s0_f, y_b, z, dsk, ng)
            new_f.append(s_f.reshape(n_p, heads, SSD_HEAD_DIM, D_STATE))
            new_b.append(s_b.reshape(n_p, heads, SSD_HEAD_DIM, D_STATE))
            x = _matmul_gate_res(geom, y, ssd_w_out[j].astype(BF16), 0, x, mods, 2, 512, "ssd_out")
        else:
            w_qkv = attn_w_qkv[j].astype(BF16)
            cos, sin = _rope_tables(len_s)
            kv_w = N_KV_HEADS * ATTN_HEAD_DIM
            past = cache_k.shape[2]
            q, k_c, v_c, k_lat, v_lat, k_ctx, v_ctx = _attn_qkv(
                geom, x, mods, norm_mix[i], w_qkv, cos, sin, attn_q_norm[j], attn_k_norm[j], n_heads, past)
            k_lat, v_lat = _append_cache(k_lat, v_lat, cache_k[:, j].reshape(n_s, past, kv_w),
                                         cache_v[:, j].reshape(n_s, past, kv_w), len_s)
            o = _attention(q, k_c, v_c, k_lat, v_lat, geom, rep)
            new_k.append(k_ctx.reshape(n_p, len_p, N_KV_HEADS, ATTN_HEAD_DIM))
            new_v.append(v_ctx.reshape(n_p, len_p, N_KV_HEADS, ATTN_HEAD_DIM))
            x = _matmul_gate_res(geom, o, attn_w_o[j].astype(BF16), 0, x, mods, 2, 1024, "attn_out")
        hidden = _norm_mod_matmul(geom, x, mods, norm_ffn[i], w_gu_all, i, [0, d_ff], d_ff, 512, 3, "swiglu", BF16,
                                  "ffn_up")
        x = _matmul_gate_res(geom, hidden, w_down_all, i, x, mods, 5, 512, "ffn_down")

    y_prompt = _final_norm(x, final_norm, 0, geom.t_p, "final_norm_ctx").reshape(n_p, len_p, d)
    y_sample = _final_norm(x, final_norm, geom.t_p, geom.t_s, "final_norm_lat").reshape(n_s, len_s, d)
    return (y_prompt, y_sample, jnp.stack(new_f, axis=1), jnp.stack(new_b, axis=1),
            jnp.stack(new_k, axis=1), jnp.stack(new_v, axis=1))
```

```python
import functools
import math

import jax
import jax.numpy as jnp
from jax import lax
from jax.experimental import pallas as pl
from jax.experimental.pallas import tpu as pltpu

F32 = jnp.float32
BF16 = jnp.bfloat16

EPS = 1e-6
GRID_W = 64
SSD_HEAD_DIM = 64
SSD_GROUPS = 8
D_STATE = 128
D_CONV = 5
CHUNK = 128
ATTN_HEAD_DIM = 128
N_KV_HEADS = 4
ROPE_THETA = 10000.0

VMEM_LIMIT_BYTES = 56 * 1024 * 1024
MAX_TOKEN_TILE = 1024
SCAN_ROWS = 2 * CHUNK
CONV_ROWS = 256
CONV_HALO = 16
ATTN_Q_ROWS = 256
ATTN_LAT_Q_ROWS = 512
ATTN_CHAIN_ROWS = 256
ATTN_KEY_BLOCK = 256
SCAN_GROUPS_PER_STEP = 8
FINAL_ROWS = 256
PROLOGUE_ROWS = 256
QKV_EPILOGUE_ROWS = 128
LOG2E = math.log2(math.e)


def _params(n_axes):
    return pltpu.CompilerParams(dimension_semantics=("arbitrary",) * n_axes,
                                vmem_limit_bytes=VMEM_LIMIT_BYTES)


def _silu(v):
    return v * jax.nn.sigmoid(v)


def _softplus(v):
    return jnp.maximum(v, 0.0) + jnp.log1p(jnp.exp(-jnp.abs(v)))


def _dot(a, b):
    return jnp.dot(a, b, preferred_element_type=F32)


def _split2(v):
    hi = v.astype(BF16)
    lo = (v - hi.astype(F32)).astype(BF16)
    return hi, lo


def _split3(v):
    hi = v.astype(BF16)
    r = v - hi.astype(F32)
    mid = r.astype(BF16)
    lo = (r - mid.astype(F32)).astype(BF16)
    return hi, mid, lo


def _adaln_kernel(c_ref, w_ref, b_ref, o_ref):
    s = _silu(c_ref[...]).astype(BF16)
    o_ref[0] = _dot(s, w_ref[0].astype(BF16)) + b_ref[0]


def _adaln(cond, w_mod, b_mod):
    depth, d, n = w_mod.shape
    rows = cond.shape[0]
    tn = 1536
    assert n % tn == 0
    return pl.pallas_call(
        _adaln_kernel,
        grid=(depth, n // tn),
        in_specs=[pl.BlockSpec((rows, d), lambda l, j: (0, 0)),
                  pl.BlockSpec((1, d, tn), lambda l, j: (l, 0, j)),
                  pl.BlockSpec((1, 1, tn), lambda l, j: (l, 0, j))],
        out_specs=pl.BlockSpec((1, rows, tn), lambda l, j: (l, 0, j)),
        out_shape=jax.ShapeDtypeStruct((depth, rows, n), F32),
        compiler_params=_params(2),
        name="adaln",
    )(cond, w_mod, b_mod.reshape(depth, 1, n))


class _Geom:
    def __init__(self, n_p, len_p, n_s, len_s):
        self.n_p, self.len_p, self.n_s, self.len_s = n_p, len_p, n_s, len_s
        self.t_p = n_p * len_p
        self.t_s = n_s * len_s
        self.t = self.t_p + self.t_s
        self.tm = math.gcd(math.gcd(self.t_p, len_s), MAX_TOKEN_TILE)

    def mod_row(self, i, tile):
        npt = self.t_p // tile
        per_seq = self.len_s // tile
        return jnp.where(i < npt, 0, 1 + (i - npt) // per_seq)


def _row_blocks(n_rows, block):
    return [slice(r0, r0 + block) for r0 in range(0, n_rows, block)]


def _norm_mod_rows(x_ref, mod_ref, g_ref, rows, shift_row):
    x = x_ref[rows, :]
    y = x * lax.rsqrt(jnp.mean(x * x, axis=-1, keepdims=True) + EPS) * g_ref[...]
    scale1 = 1.0 + mod_ref[0, shift_row + 1:shift_row + 2, :]
    return (y * scale1 + mod_ref[0, shift_row:shift_row + 1, :]).astype(BF16)


def _ssd_in_kernel(mod_ref, g_ref, w_ref, wdt_ref, *rest, z_blocks, sub_rows, n_p_tiles):
    x_refs = rest[:-4]
    z_ref, xbc_ref, dt_ref, h_ref = rest[-4:]
    j = pl.program_id(1)
    first = j == 0

    def first_step(x_ref):
        for rows in _row_blocks(z_ref.shape[0], sub_rows):
            h = _norm_mod_rows(x_ref, mod_ref, g_ref, rows, 0)
            h_ref[rows, :] = h
            z_ref[rows, :] = _dot(h, w_ref[...]).astype(z_ref.dtype)

    if len(x_refs) == 1:
        pl.when(first)(lambda: first_step(x_refs[0]))
    else:
        is_p = pl.program_id(0) < n_p_tiles
        pl.when(jnp.logical_and(first, is_p))(lambda: first_step(x_refs[0]))
        pl.when(jnp.logical_and(first, jnp.logical_not(is_p)))(lambda: first_step(x_refs[1]))

    @pl.when(jnp.logical_not(first))
    def _():
        h = h_ref[...]
        acc = _dot(h, w_ref[...])

        @pl.when(j < z_blocks)
        def _():
            z_ref[...] = acc.astype(z_ref.dtype)

        @pl.when(j >= z_blocks)
        def _():
            xbc_ref[...] = acc.astype(xbc_ref.dtype)

        @pl.when(j == pl.num_programs(1) - 1)
        def _():
            dt_ref[...] = _dot(h, wdt_ref[...])


def _ssd_in_proj(geom, x, mods, gain, w, d_inner, conv_dim, n_dt, tn):
    t, d = geom.t, w.shape[0]
    tm = geom.tm
    assert d_inner % tn == 0 and conv_dim % tn == 0 and (d_inner + conv_dim) % n_dt == 0
    z_blocks, xbc_blocks = d_inner // tn, conv_dim // tn
    dt_block = (d_inner + conv_dim) // n_dt
    xs, x_specs = _residual_specs(geom, x, (tm, d), lambda j: 0, single_buffer=True)
    return pl.pallas_call(
        functools.partial(_ssd_in_kernel, z_blocks=z_blocks, sub_rows=min(tm, PROLOGUE_ROWS),
                          n_p_tiles=geom.t_p // tm),
        grid=(t // tm, z_blocks + xbc_blocks),
        in_specs=[pl.BlockSpec((1, 6, d), lambda i, j: (geom.mod_row(i, tm), 0, 0)),
                  pl.BlockSpec((1, d), lambda i, j: (0, 0)),
                  pl.BlockSpec((d, tn), lambda i, j: (0, j)),
                  pl.BlockSpec((d, n_dt), lambda i, j: (0, dt_block))] + x_specs,
        out_specs=[pl.BlockSpec((tm, tn), lambda i, j: (i, jnp.minimum(j, z_blocks - 1))),
                   pl.BlockSpec((tm, tn), lambda i, j: (i, jnp.maximum(j - z_blocks, 0))),
                   pl.BlockSpec((tm, n_dt), lambda i, j: (i, 0))],
        out_shape=[jax.ShapeDtypeStruct((t, d_inner), BF16),
                   jax.ShapeDtypeStruct((t, conv_dim), BF16),
                   jax.ShapeDtypeStruct((t, n_dt), F32)],
        scratch_shapes=[pltpu.VMEM((tm, d), BF16)],
        compiler_params=_params(2),
        name="ssd_in",
    )(mods, gain.reshape(1, d), w, w, *xs)


def _weight_spec(w, layer, k, tn, col_block0):
    if w.ndim == 2:
        return pl.BlockSpec((k, tn), lambda i, j: (0, col_block0 + j))
    return pl.BlockSpec((None, k, tn), lambda i, j: (layer, 0, col_block0 + j))


def _nm_kernel(x_ref, mod_ref, g_ref, *rest, n_w, shift_row, act, sub_rows):
    w_refs = rest[:n_w]
    o_ref = rest[n_w]
    h_ref = rest[n_w + 1]

    def emit(rows, h):
        if act == "swiglu":
            gate = _dot(h, w_refs[0][...])
            up = _dot(h, w_refs[1][...])
            o_ref[rows, :] = (_silu(gate) * up).astype(o_ref.dtype)
        else:
            o_ref[rows, :] = _dot(h, w_refs[0][...]).astype(o_ref.dtype)

    first = pl.program_id(1) == 0

    @pl.when(first)
    def _():
        for rows in _row_blocks(o_ref.shape[0], sub_rows):
            h = _norm_mod_rows(x_ref, mod_ref, g_ref, rows, shift_row)
            h_ref[rows, :] = h
            emit(rows, h)

    @pl.when(jnp.logical_not(first))
    def _():
        emit(slice(None), h_ref[...])


def _norm_mod_matmul(geom, x, mods, gain, w, layer, col_offsets, n_cols, tn, shift_row, act, out_dtype, name):
    t, d = x.shape
    tm = geom.tm
    assert n_cols % tn == 0 and all(off % tn == 0 for off in col_offsets)
    w_specs = [_weight_spec(w, layer, d, tn, off // tn) for off in col_offsets]
    return pl.pallas_call(
        functools.partial(_nm_kernel, n_w=len(col_offsets), shift_row=shift_row, act=act,
                          sub_rows=min(tm, PROLOGUE_ROWS)),
        grid=(t // tm, n_cols // tn),
        in_specs=[pl.BlockSpec((tm, d), lambda i, j: (i, 0)),
                  pl.BlockSpec((1, 6, d), lambda i, j: (geom.mod_row(i, tm), 0, 0)),
                  pl.BlockSpec((1, d), lambda i, j: (0, 0))] + w_specs,
        out_specs=pl.BlockSpec((tm, tn), lambda i, j: (i, j)),
        out_shape=jax.ShapeDtypeStruct((t, n_cols), out_dtype),
        scratch_shapes=[pltpu.VMEM((tm, d), BF16)],
        compiler_params=_params(2),
        name=name,
    )(x, mods, gain.reshape(1, d), *([w] * len(col_offsets)))


def _gate_res_kernel(a_ref, w_ref, mod_ref, *rest, gate_row, n_p_tiles):
    x_refs, o_ref = rest[:-1], rest[-1]
    gated = mod_ref[0, gate_row:gate_row + 1, :] * _dot(a_ref[...], w_ref[...])
    if len(x_refs) == 1:
        o_ref[...] = x_refs[0][...] + gated
    else:
        is_p = pl.program_id(0) < n_p_tiles

        @pl.when(is_p)
        def _():
            o_ref[...] = x_refs[0][...] + gated

        @pl.when(jnp.logical_not(is_p))
        def _():
            o_ref[...] = x_refs[1][...] + gated


def _residual_specs(geom, x, block, col_of, single_buffer=False):
    tm = geom.tm
    npt = geom.t_p // tm
    if not isinstance(x, tuple):
        return [x], [pl.BlockSpec(block, lambda i, j: (i, col_of(j)))]
    mode = dict(pipeline_mode=pl.Buffered(1)) if single_buffer else {}
    return list(x), [pl.BlockSpec(block, lambda i, j: (jnp.minimum(i, npt - 1), col_of(j)), **mode),
                     pl.BlockSpec(block, lambda i, j: (jnp.maximum(i - npt, 0), col_of(j)))]


def _matmul_gate_res(geom, a, w, layer, x, mods, gate_row, tn, name):
    t, k = a.shape
    d = w.shape[-1]
    tm = geom.tm
    assert d % tn == 0
    xs, x_specs = _residual_specs(geom, x, (tm, tn), lambda j: j)
    return pl.pallas_call(
        functools.partial(_gate_res_kernel, gate_row=gate_row, n_p_tiles=geom.t_p // tm),
        grid=(t // tm, d // tn),
        in_specs=[pl.BlockSpec((tm, k), lambda i, j: (i, 0)),
                  _weight_spec(w, layer, k, tn, 0),
                  pl.BlockSpec((1, 6, tn), lambda i, j: (geom.mod_row(i, tm), 0, j))] + x_specs,
        out_specs=pl.BlockSpec((tm, tn), lambda i, j: (i, j)),
        out_shape=jax.ShapeDtypeStruct((t, d), F32),
        compiler_params=_params(2),
        name=name,
    )(a, w, mods, *xs)


def _gate_res_norm_kernel(a_ref, w_ref, x_ref, mod_ref, g_ref, yp_ref, ys_ref, *, gate_row, n_p_tiles):
    x = x_ref[...] + mod_ref[0, gate_row:gate_row + 1, :] * _dot(a_ref[...], w_ref[...])
    y = x * lax.rsqrt(jnp.mean(x * x, axis=-1, keepdims=True) + EPS) * g_ref[...]
    is_p = pl.program_id(0) < n_p_tiles

    @pl.when(is_p)
    def _():
        yp_ref[...] = y

    @pl.when(jnp.logical_not(is_p))
    def _():
        ys_ref[...] = y


def _matmul_gate_res_norm(geom, a, w, layer, x, mods, gate_row, gain):
    t, k = a.shape
    d = w.shape[-1]
    tm = math.gcd(math.gcd(geom.t_p, geom.len_s), FINAL_ROWS)
    npt = geom.t_p // tm
    w_block = (k, d) if w.ndim == 2 else (None, k, d)
    w_index = (lambda i: (0, 0)) if w.ndim == 2 else (lambda i: (layer, 0, 0))
    return pl.pallas_call(
        functools.partial(_gate_res_norm_kernel, gate_row=gate_row, n_p_tiles=npt),
        grid=(t // tm,),
        in_specs=[pl.BlockSpec((tm, k), lambda i: (i, 0)),
                  pl.BlockSpec(w_block, w_index, pipeline_mode=pl.Buffered(1)),
                  pl.BlockSpec((tm, d), lambda i: (i, 0)),
                  pl.BlockSpec((1, 6, d), lambda i: (geom.mod_row(i, tm), 0, 0)),
                  pl.BlockSpec((1, d), lambda i: (0, 0))],
        out_specs=[pl.BlockSpec((tm, d), lambda i: (jnp.minimum(i, npt - 1), 0)),
                   pl.BlockSpec((tm, d), lambda i: (jnp.maximum(i - npt, 0), 0))],
        out_shape=[jax.ShapeDtypeStruct((geom.t_p, d), F32),
                   jax.ShapeDtypeStruct((geom.t_s, d), F32)],
        compiler_params=_params(1),
        name="ffn_down_final",
    )(a, w, x, mods, gain.reshape(1, d))


def _conv_kernel(u_ref, prev_ref, next_ref, w_ref, b_ref, o_ref, ext_ref, *, n_p_blocks, p_blocks, s_blocks):
    i = pl.program_id(0)
    lb = u_ref.shape[0]
    is_p = i < n_p_blocks
    pos = jnp.where(is_p, i % p_blocks, (i - n_p_blocks) % s_blocks)
    last = jnp.where(is_p, p_blocks, s_blocks) - 1
    ext_ref[0:CONV_HALO, :] = jnp.where(pos > 0, prev_ref[...].astype(F32), 0.0)
    ext_ref[CONV_HALO:CONV_HALO + lb, :] = u_ref[...].astype(F32)
    ext_ref[CONV_HALO + lb:2 * CONV_HALO + lb, :] = jnp.where(pos < last, next_ref[...].astype(F32), 0.0)
    ext = ext_ref[...]
    n_ext = ext.shape[0]
    centre = D_CONV // 2
    acc = b_ref[...] + w_ref[centre:centre + 1, :] * ext[CONV_HALO:CONV_HALO + lb, :]
    for k in range(D_CONV):
        if k != centre:
            shifted = pltpu.roll(ext, (centre - k) % n_ext, 0)
            acc = acc + w_ref[k:k + 1, :] * shifted[CONV_HALO:CONV_HALO + lb, :]
    o_ref[...] = _silu(acc).astype(o_ref.dtype)


def _ssd_conv(geom, xbc, conv_w, conv_b):
    t, c = xbc.shape
    lb, ct = CONV_ROWS, 2048
    assert geom.len_p % lb == 0 and geom.len_s % lb == 0 and c % ct == 0
    hb = lb // CONV_HALO
    n_halo_blocks = t // CONV_HALO
    kern = functools.partial(_conv_kernel, n_p_blocks=geom.t_p // lb, p_blocks=geom.len_p // lb,
                             s_blocks=geom.len_s // lb)
    return pl.pallas_call(
        kern,
        grid=(t // lb, c // ct),
        in_specs=[pl.BlockSpec((lb, ct), lambda i, j: (i, j)),
                  pl.BlockSpec((CONV_HALO, ct), lambda i, j: (jnp.maximum(i * hb - 1, 0), j)),
                  pl.BlockSpec((CONV_HALO, ct), lambda i, j: (jnp.minimum((i + 1) * hb, n_halo_blocks - 1), j)),
                  pl.BlockSpec((D_CONV, ct), lambda i, j: (0, j)),
                  pl.BlockSpec((1, ct), lambda i, j: (0, j))],
        out_specs=pl.BlockSpec((lb, ct), lambda i, j: (i, j)),
        out_shape=jax.ShapeDtypeStruct((t, c), BF16),
        scratch_shapes=[pltpu.VMEM((lb + 2 * CONV_HALO, ct), F32)],
        compiler_params=_params(2),
        name="ssd_conv",
    )(xbc, xbc, xbc, conv_w, conv_b.reshape(1, c))


def _dt_prep_kernel(dt_ref, alog_ref, bias_ref, cs_ref, src_ref, e_ref, w_ref):
    q = CHUNK
    n_dir_heads = dt_ref.shape[1]
    li = lax.broadcasted_iota(jnp.int32, (q, q), 0)
    si = lax.broadcasted_iota(jnp.int32, (q, q), 1)
    lower = (li >= si).astype(BF16)
    upper = (li <= si).astype(BF16)
    fwd = lax.broadcasted_iota(jnp.int32, (1, n_dir_heads), 1) < n_dir_heads // 2
    a_scale = -jnp.exp(alog_ref[...]) * LOG2E
    for r0 in range(0, dt_ref.shape[0], q):
        dt = _softplus(dt_ref[r0:r0 + q, :] + bias_ref[...])
        parts = _split3(dt * a_scale)
        cs = jnp.where(fwd, sum(_dot(lower, p) for p in parts), sum(_dot(upper, p) for p in parts))
        total = jnp.where(fwd, cs[q - 1:q, :], cs[0:1, :])
        cs_ref[:, r0:r0 + q] = cs.T
        src_ref[:, r0:r0 + q] = (cs - jnp.log2(dt)).T
        e_ref[:, r0:r0 + q] = jnp.exp2(cs).T
        w_ref[:, r0:r0 + q] = (dt * jnp.exp2(total - cs)).T


def _dt_prep(dt_raw, a_log, dt_bias):
    t, nh = dt_raw.shape
    rb = 1024 if t % 1024 == 0 else SCAN_ROWS
    out = jax.ShapeDtypeStruct((nh, t), F32)
    head_major = pl.BlockSpec((nh, rb), lambda i: (0, i))
    return pl.pallas_call(
        _dt_prep_kernel,
        grid=(t // rb,),
        in_specs=[pl.BlockSpec((rb, nh), lambda i: (i, 0)),
                  pl.BlockSpec((1, nh), lambda i: (0, 0)),
                  pl.BlockSpec((1, nh), lambda i: (0, 0))],
        out_specs=[head_major] * 4,
        out_shape=[out] * 4,
        compiler_params=_params(1),
        name="ssd_dt_prep",
    )(dt_raw, a_log.reshape(1, nh), dt_bias.reshape(1, nh))


def _scan_kernel(*refs, rev, fuse_norm, n_steps, n_p_steps, p_steps, s_steps, hpg, gps):
    if fuse_norm:
        (x_ref, b_ref, c_ref, cs_ref, src_ref, e_ref, w_ref, s0_ref, yo_ref, z_ref, dsk_ref, ng_ref,
         y_ref, sfin_ref, st_ref) = refs
    else:
        (x_ref, b_ref, c_ref, cs_ref, src_ref, e_ref, w_ref, s0_ref,
         y_ref, sfin_ref, st_ref) = refs
    q = CHUNK
    hd = SSD_HEAD_DIM
    n = D_STATE
    width = hpg * hd
    step = pl.program_id(1)
    blk = (n_steps - 1 - step) if rev else step
    is_p = blk < n_p_steps
    pos = jnp.where(is_p, blk % p_steps, (blk - n_p_steps) % s_steps)
    last = jnp.where(is_p, p_steps, s_steps) - 1
    seq_start = (pos == last) if rev else (pos == 0)
    seq_end = (pos == 0) if rev else (pos == last)

    @pl.when(jnp.logical_and(seq_start, is_p))
    def _():
        st_ref[...] = jnp.zeros_like(st_ref)

    @pl.when(jnp.logical_and(seq_start, jnp.logical_not(is_p)))
    def _():
        for gi in range(gps):
            st_ref[gi] = s0_ref[gi].T

    li = lax.broadcasted_iota(jnp.int32, (q, q), 0)
    si = lax.broadcasted_iota(jnp.int32, (q, q), 1)
    keep = (li <= si) if rev else (li >= si)
    expand = (lax.broadcasted_iota(jnp.int32, (2 * hpg, width), 1) // hd
              == lax.broadcasted_iota(jnp.int32, (2 * hpg, width), 0) % hpg).astype(BF16)
    lane = lax.broadcasted_iota(jnp.int32, (q, 2 * hd), 1)
    edge = 0 if rev else q - 1

    def widen(v_r):
        hi, lo = _split2(v_r)
        return lax.dot_general(jnp.concatenate([hi, lo], axis=0), expand, (((0,), (0,)), ((), ())),
                               preferred_element_type=F32)

    def chunk(gi, r0):
        cols = slice(gi * width, (gi + 1) * width)
        heads = slice(gi * hpg, (gi + 1) * hpg)
        x = x_ref[r0:r0 + q, cols]
        bm = b_ref[r0:r0 + q, gi * n:(gi + 1) * n]
        cm = c_ref[r0:r0 + q, gi * n:(gi + 1) * n]
        cs_c = cs_ref[heads, r0:r0 + q].T
        src_r = src_ref[heads, r0:r0 + q]
        e_x = widen(e_ref[heads, r0:r0 + q])
        w_x = widen(w_ref[heads, r0:r0 + q])

        cb = lax.dot_general(cm, bm, (((1,), (1,)), ((), ())), preferred_element_type=F32)
        state = st_ref[gi]
        y = _dot(cm, state.astype(BF16)) * e_x
        parts = []
        for p in range(hpg // 2):
            ms = []
            for r in (2 * p, 2 * p + 1):
                seg = cs_c[:, r:r + 1] - src_r[r:r + 1, :]
                ms.append((cb * jnp.exp2(jnp.where(keep, seg, -jnp.inf))).astype(BF16))
            x2 = x[:, 2 * p * hd:2 * (p + 1) * hd]
            zero = jnp.zeros_like(x2)
            rhs = jnp.concatenate([jnp.where(lane < hd, x2, zero), jnp.where(lane >= hd, x2, zero)], axis=0)
            parts.append(_dot(jnp.concatenate(ms, axis=1), rhs))
        y = y + jnp.concatenate(parts, axis=1)

        xf = x.astype(F32)
        xw = (xf * w_x).astype(BF16)
        st_ref[gi] = state * e_x[edge:edge + 1, :] + lax.dot_general(
            bm, xw, (((0,), (0,)), ((), ())), preferred_element_type=F32)

        if fuse_norm:
            y = y + yo_ref[r0:r0 + q, cols].astype(F32) + dsk_ref[gi] * xf
            yz = y * _silu(z_ref[r0:r0 + q, cols].astype(F32))
            yz = yz * lax.rsqrt(jnp.mean(yz * yz, axis=-1, keepdims=True) + EPS)
            y = yz * ng_ref[gi]
        y_ref[r0:r0 + q, cols] = y.astype(y_ref.dtype)

    n_chunks = x_ref.shape[0] // q
    for ci in (range(n_chunks - 1, -1, -1) if rev else range(n_chunks)):
        for gi in range(gps):
            chunk(gi, ci * q)

    @pl.when(jnp.logical_and(seq_end, is_p))
    def _():
        for gi in range(gps):
            sfin_ref[gi] = st_ref[gi].T


def _ssd_scan(geom, direction, xbc, decays, s0, y_other=None, z=None, dsk=None, ng=None):
    t = xbc.shape[0]
    g_n = SSD_GROUPS
    gps = SCAN_GROUPS_PER_STEP
    hpg = decays[0].shape[0] // (2 * g_n)
    width = hpg * SSD_HEAD_DIM
    n = D_STATE
    rb = SCAN_ROWS
    assert geom.len_p % rb == 0 and geom.len_s % rb == 0 and g_n % gps == 0
    n_steps = t // rb
    n_p_steps, p_steps, s_steps = geom.t_p // rb, geom.len_p // rb, geom.len_s // rb
    rev = direction == 1
    fuse = y_other is not None
    x_cols = g_n * width
    b_blk0 = x_cols // (gps * n)
    c_blk0 = (x_cols + g_n * n) // (gps * n)
    head_blk0 = direction * (g_n // gps)

    def blk(s):
        return (n_steps - 1 - s) if rev else s

    def s0_idx(g, s):
        return (jnp.clip((blk(s) - n_p_steps) // s_steps, 0, geom.n_s - 1), g, 0, 0)

    def sfin_idx(g, s):
        return (jnp.minimum(blk(s) // p_steps, geom.n_p - 1), g, 0, 0)

    wide = pl.BlockSpec((rb, gps * width), lambda g, s: (blk(s), g))
    head_major = pl.BlockSpec((gps * hpg, rb), lambda g, s: (head_blk0 + g, blk(s)))
    in_specs = [wide,
                pl.BlockSpec((rb, gps * n), lambda g, s: (blk(s), b_blk0 + g)),
                pl.BlockSpec((rb, gps * n), lambda g, s: (blk(s), c_blk0 + g)),
                head_major, head_major, head_major, head_major,
                pl.BlockSpec((None, gps, width, n), s0_idx)]
    args = [xbc, xbc, xbc, *decays, s0]
    if fuse:
        in_specs += [wide, wide,
                     pl.BlockSpec((gps, 1, width), lambda g, s: (g, 0, 0)),
                     pl.BlockSpec((gps, 1, width), lambda g, s: (g, 0, 0))]
        args += [y_other, z, dsk, ng]
    kern = functools.partial(_scan_kernel, rev=rev, fuse_norm=fuse, n_steps=n_steps, n_p_steps=n_p_steps,
                             p_steps=p_steps, s_steps=s_steps, hpg=hpg, gps=gps)
    return pl.pallas_call(
        kern,
        grid=(g_n // gps, n_steps),
        in_specs=in_specs,
        out_specs=[wide, pl.BlockSpec((None, gps, width, n), sfin_idx)],
        out_shape=[jax.ShapeDtypeStruct((t, x_cols), BF16),
                   jax.ShapeDtypeStruct((geom.n_p, g_n, width, n), F32)],
        scratch_shapes=[pltpu.VMEM((gps, n, width), F32)],
        compiler_params=_params(2),
        name="ssd_scan_bwd" if rev else "ssd_scan_fwd",
    )(*args)


def _attn_qkv_kernel(x_ref, mod_ref, g_ref, w_ref, cos_ref, sin_ref, qg_ref, kg_ref,
                     q_ref, kc_ref, vc_ref, kl_ref, vl_ref, kf_ref, vf_ref, h_ref,
                     *, n_p_tiles, q_blocks, sub_rows):
    i = pl.program_id(0)
    j = pl.program_id(1)
    hd = ATTN_HEAD_DIM
    is_p = i < n_p_tiles
    is_s = jnp.logical_not(is_p)
    first = j == 0
    tm, tn = q_ref.shape
    heads = tn // hd
    blocks = _row_blocks(tm, sub_rows)
    lane = lax.broadcasted_iota(jnp.int32, (sub_rows, hd), 1)
    first_half = (lane % (hd // 2)) < (hd // 4)

    def norm(v, gain):
        return v * lax.rsqrt(jnp.mean(v * v, axis=-1, keepdims=True) + EPS) * gain

    def rope(v, c, s):
        partner = jnp.where(first_half, pltpu.roll(v, hd - hd // 4, 1), pltpu.roll(v, hd // 4, 1))
        return v * c + partner * s

    def q_rows(rows, h):
        q_scale = hd ** -0.5 * LOG2E
        acc = _dot(h, w_ref[...])
        cos_q = jnp.where(is_p, 1.0, cos_ref[rows, :]) * q_scale
        sin_q = jnp.where(is_p, 0.0, sin_ref[rows, :]) * q_scale
        for hh in range(heads):
            cols = slice(hh * hd, (hh + 1) * hd)
            q_ref[rows, cols] = rope(norm(acc[:, cols], qg_ref[...]), cos_q, sin_q).astype(q_ref.dtype)

    @pl.when(first)
    def _():
        for rows in blocks:
            h = _norm_mod_rows(x_ref, mod_ref, g_ref, rows, 0)
            h_ref[rows, :] = h
            q_rows(rows, h)

    @pl.when(jnp.logical_and(jnp.logical_not(first), j < q_blocks))
    def _():
        for rows in blocks:
            q_rows(rows, h_ref[rows, :])

    @pl.when(jnp.logical_and(j == q_blocks, is_p))
    def _():
        for rows in blocks:
            acc = _dot(h_ref[rows, :], w_ref[...])
            for hh in range(heads):
                cols = slice(hh * hd, (hh + 1) * hd)
                normed = norm(acc[:, cols], kg_ref[...])
                kf_ref[rows, cols] = normed
                kc_ref[rows, cols] = normed.astype(kc_ref.dtype)

    @pl.when(jnp.logical_and(j == q_blocks, is_s))
    def _():
        for rows in blocks:
            acc = _dot(h_ref[rows, :], w_ref[...])
            for hh in range(heads):
                cols = slice(hh * hd, (hh + 1) * hd)
                normed = norm(acc[:, cols], kg_ref[...])
                kl_ref[rows, cols] = rope(normed, cos_ref[rows, :], sin_ref[rows, :]).astype(kl_ref.dtype)

    @pl.when(jnp.logical_and(j == q_blocks + 1, is_p))
    def _():
        acc = _dot(h_ref[...], w_ref[...])
        vf_ref[...] = acc
        vc_ref[...] = acc.astype(vc_ref.dtype)

    @pl.when(jnp.logical_and(j == q_blocks + 1, is_s))
    def _():
        vl_ref[...] = _dot(h_ref[...], w_ref[...]).astype(vl_ref.dtype)


def _attn_qkv(geom, x, mods, gain, w, cos, sin, q_gain, k_gain, n_heads, past):
    t, d = x.shape
    tm = geom.tm
    hd, n_kv = ATTN_HEAD_DIM, N_KV_HEADS
    tn = n_kv * hd
    q_cols = n_heads * hd
    assert q_cols % tn == 0 and q_cols >= tn and w.shape[1] == q_cols + 2 * tn
    q_blocks = q_cols // tn
    npt = geom.t_p // tm
    per_seq = geom.len_s // tm

    def tab_idx(i, j):
        return (jnp.where(i < npt, 0, (i - npt) % per_seq), 0)

    def ctx_idx(i, j):
        return (jnp.minimum(i, npt - 1), 0)

    def lat_idx(i, j):
        return (jnp.clip((i - npt) // per_seq, 0, geom.n_s - 1), jnp.where(i < npt, 0, (i - npt) % per_seq), 0)

    kv_lat = jax.ShapeDtypeStruct((geom.n_s, geom.len_s + past, tn), BF16)
    return pl.pallas_call(
        functools.partial(_attn_qkv_kernel, n_p_tiles=npt, q_blocks=q_blocks,
                          sub_rows=min(tm, QKV_EPILOGUE_ROWS)),
        grid=(t // tm, q_blocks + 2),
        in_specs=[pl.BlockSpec((tm, d), lambda i, j: (i, 0)),
                  pl.BlockSpec((1, 6, d), lambda i, j: (geom.mod_row(i, tm), 0, 0)),
                  pl.BlockSpec((1, d), lambda i, j: (0, 0)),
                  pl.BlockSpec((d, tn), lambda i, j: (0, j)),
                  pl.BlockSpec((tm, hd), tab_idx),
                  pl.BlockSpec((tm, hd), tab_idx),
                  pl.BlockSpec((1, hd), lambda i, j: (0, 0)),
                  pl.BlockSpec((1, hd), lambda i, j: (0, 0))],
        out_specs=[pl.BlockSpec((tm, tn), lambda i, j: (i, jnp.minimum(j, q_blocks - 1))),
                   pl.BlockSpec((tm, tn), ctx_idx),
                   pl.BlockSpec((tm, tn), ctx_idx),
                   pl.BlockSpec((None, tm, tn), lat_idx),
                   pl.BlockSpec((None, tm, tn), lat_idx),
                   pl.BlockSpec((tm, tn), ctx_idx),
                   pl.BlockSpec((tm, tn), ctx_idx)],
        out_shape=[jax.ShapeDtypeStruct((t, q_cols), BF16),
                   jax.ShapeDtypeStruct((geom.t_p, tn), BF16),
                   jax.ShapeDtypeStruct((geom.t_p, tn), BF16),
                   kv_lat, kv_lat,
                   jax.ShapeDtypeStruct((geom.t_p, tn), F32),
                   jax.ShapeDtypeStruct((geom.t_p, tn), F32)],
        scratch_shapes=[pltpu.VMEM((tm, d), BF16)],
        compiler_params=_params(2),
        name="attn_qkv",
    )(x, mods, gain.reshape(1, d), w, cos, sin, q_gain.reshape(1, hd), k_gain.reshape(1, hd))


def _append_cache_kernel(ck_ref, cv_ref, k_in, v_in, k_ref, v_ref):
    k_ref[...] = ck_ref[...].astype(k_ref.dtype)
    v_ref[...] = cv_ref[...].astype(v_ref.dtype)


def _append_cache(k_lat, v_lat, cache_k, cache_v, len_s):
    n_s, past, kv = cache_k.shape
    assert len_s % past == 0
    cache_spec = pl.BlockSpec((None, past, kv), lambda b: (b, 0, 0))
    tail_spec = pl.BlockSpec((None, past, kv), lambda b: (b, len_s // past, 0))
    any_spec = pl.BlockSpec(memory_space=pl.ANY)
    return pl.pallas_call(
        _append_cache_kernel,
        grid=(n_s,),
        in_specs=[cache_spec, cache_spec, any_spec, any_spec],
        out_specs=[tail_spec, tail_spec],
        out_shape=[jax.ShapeDtypeStruct(k_lat.shape, k_lat.dtype)] * 2,
        input_output_aliases={2: 0, 3: 1},
        compiler_params=_params(1),
        name="append_cache",
    )(cache_k, cache_v, k_lat, v_lat)


def _attn_kernel(*refs, rep, aliased, heads_per_chain):
    q_ref, k_ref, v_ref = refs[0], refs[1], refs[2]
    o_ref = refs[4] if aliased else refs[3]
    hd = ATTN_HEAD_DIM
    rows = q_ref.shape[0]
    k = k_ref[...]
    v = v_ref[...]
    for c0 in range(0, rep, heads_per_chain):
        heads = range(c0, c0 + heads_per_chain)
        q = jnp.concatenate([q_ref[:, r * hd:(r + 1) * hd] for r in heads], axis=0)
        s = lax.dot_general(q, k, (((1,), (1,)), ((), ())), preferred_element_type=F32)
        p = jnp.exp2(s - jnp.max(s, axis=-1, keepdims=True))
        denom = jnp.sum(p, axis=-1, keepdims=True)
        o = _dot(p.astype(BF16), v) / denom
        for n, r in enumerate(heads):
            o_ref[:, r * hd:(r + 1) * hd] = o[n * rows:(n + 1) * rows, :].astype(o_ref.dtype)


def _attn_stream_kernel(q_ref, k_ref, v_ref, o_in, o_ref, *, rep, chain_rows, key_block):
    del o_in
    hd = ATTN_HEAD_DIM
    n_keys = k_ref.shape[0]
    for r in range(rep):
        cols = slice(r * hd, (r + 1) * hd)
        for rows in _row_blocks(q_ref.shape[0], chain_rows):
            q = q_ref[rows, cols]
            m = jnp.full((chain_rows, 1), -jnp.inf, F32)
            denom = jnp.zeros((chain_rows, 1), F32)
            acc = jnp.zeros((chain_rows, hd), F32)
            for c0 in range(0, n_keys, key_block):
                keys = slice(c0, c0 + key_block)
                s = lax.dot_general(q, k_ref[keys, :], (((1,), (1,)), ((), ())), preferred_element_type=F32)
                m_new = jnp.maximum(m, jnp.max(s, axis=-1, keepdims=True))
                alpha = jnp.exp2(m - m_new)
                p = jnp.exp2(s - m_new)
                denom = alpha * denom + jnp.sum(p, axis=-1, keepdims=True)
                acc = alpha * acc + _dot(p.astype(BF16), v_ref[keys, :])
                m = m_new
            o_ref[rows, cols] = (acc / denom).astype(o_ref.dtype)


def _attention(q, k_ctx, v_ctx, k_lat, v_lat, geom, rep):
    t, qd = q.shape
    hd, n_kv = ATTN_HEAD_DIM, N_KV_HEADS
    qb = ATTN_Q_ROWS
    assert geom.len_p % qb == 0 and geom.len_s % qb == 0
    out_shape = jax.ShapeDtypeStruct((t, qd), BF16)
    blk_q = (qb, rep * hd)
    p_blocks = geom.len_p // qb
    o = pl.pallas_call(
        functools.partial(_attn_kernel, rep=rep, aliased=False, heads_per_chain=rep),
        grid=(geom.n_p, n_kv, p_blocks),
        in_specs=[pl.BlockSpec(blk_q, lambda b, h, i: (b * p_blocks + i, h)),
                  pl.BlockSpec((geom.len_p, hd), lambda b, h, i: (b, h)),
                  pl.BlockSpec((geom.len_p, hd), lambda b, h, i: (b, h))],
        out_specs=pl.BlockSpec(blk_q, lambda b, h, i: (b * p_blocks + i, h)),
        out_shape=out_shape,
        compiler_params=_params(3),
        name="attn_ctx",
    )(q, k_ctx, v_ctx)
    qb = math.gcd(math.gcd(geom.len_s, geom.t_p), ATTN_LAT_Q_ROWS)
    blk_q = (qb, rep * hd)
    s_blocks = geom.len_s // qb
    q0 = geom.t_p // qb
    l_all = k_lat.shape[1]
    return pl.pallas_call(
        functools.partial(_attn_stream_kernel, rep=rep, chain_rows=math.gcd(qb, ATTN_CHAIN_ROWS),
                          key_block=math.gcd(l_all, ATTN_KEY_BLOCK)),
        grid=(geom.n_s, n_kv, s_blocks),
        in_specs=[pl.BlockSpec(blk_q, lambda b, h, i: (q0 + b * s_blocks + i, h)),
                  pl.BlockSpec((None, l_all, hd), lambda b, h, i: (b, 0, h)),
                  pl.BlockSpec((None, l_all, hd), lambda b, h, i: (b, 0, h)),
                  pl.BlockSpec(memory_space=pl.ANY)],
        out_specs=pl.BlockSpec(blk_q, lambda b, h, i: (q0 + b * s_blocks + i, h)),
        out_shape=out_shape,
        input_output_aliases={3: 0},
        compiler_params=_params(3),
        name="attn_lat",
    )(q, k_lat, v_lat, o)


def _rope_tables(length):
    half = ATTN_HEAD_DIM // 2
    pos = jnp.arange(length, dtype=jnp.int32)
    row_pos = (pos // GRID_W).astype(F32)
    col_pos = (pos % GRID_W).astype(F32)
    inv = ROPE_THETA ** (-jnp.arange(0, half, 2, dtype=F32) / half)
    ang_r = row_pos[:, None] * inv[None, :]
    ang_c = col_pos[:, None] * inv[None, :]
    cos = jnp.concatenate([jnp.cos(ang_r)] * 2 + [jnp.cos(ang_c)] * 2, axis=-1)
    sin = jnp.concatenate([-jnp.sin(ang_r), jnp.sin(ang_r), -jnp.sin(ang_c), jnp.sin(ang_c)], axis=-1)
    return cos, sin


def kernel(x_prompt, x_sample, state_ssd_fwd, state_ssd_bwd, cache_k, cache_v, c, c_ctx, w_mod, b_mod, norm_mix, norm_ffn, ssd_w_in, ssd_conv_w, ssd_conv_b, ssd_a_log, ssd_dt_bias, ssd_d, ssd_norm, ssd_w_out, attn_w_qkv, attn_q_norm, attn_k_norm, attn_w_o, ffn_w_gu, ffn_w_down, final_norm):
    n_p, len_p, d = x_prompt.shape
    n_s, len_s, _ = x_sample.shape
    depth = w_mod.shape[0]
    geom = _Geom(n_p, len_p, n_s, len_s)
    d_ff = ffn_w_down.shape[1]
    d_inner = ssd_w_out.shape[1]
    heads = d_inner // SSD_HEAD_DIM
    hpg = heads // SSD_GROUPS
    gw = hpg * SSD_HEAD_DIM
    conv_dim = d_inner + 2 * SSD_GROUPS * D_STATE
    n_heads = attn_w_o.shape[1] // ATTN_HEAD_DIM
    rep = n_heads // N_KV_HEADS

    x = (x_prompt.reshape(geom.t_p, d), x_sample.reshape(geom.t_s, d))
    w_gu_all = ffn_w_gu.astype(BF16)
    w_down_all = ffn_w_down.astype(BF16)

    cond_rows = -(-(1 + n_s) // 8) * 8
    cond = jnp.zeros((cond_rows, d), F32).at[0].set(c_ctx).at[1:1 + n_s].set(c)
    mods_all = _adaln(cond, w_mod, b_mod).reshape(depth, cond_rows, 6, d)

    new_f, new_b, new_k, new_v = [], [], [], []
    for i in range(depth):
        mods = mods_all[i]
        j = i // 2
        if i % 2 == 0:
            w_in = ssd_w_in[j].astype(BF16)
            z, xbc, dt_raw = _ssd_in_proj(geom, x, mods, norm_mix[i], w_in, d_inner, conv_dim, 2 * heads, 1024)
            xbc = _ssd_conv(geom, xbc, ssd_conv_w[j], ssd_conv_b[j])
            decays = _dt_prep(dt_raw, ssd_a_log[j], ssd_dt_bias[j])
            dsk = jnp.repeat(ssd_d[j, 0] + ssd_d[j, 1], SSD_HEAD_DIM).reshape(SSD_GROUPS, 1, gw)
            ng = ssd_norm[j].reshape(SSD_GROUPS, 1, gw)
            s0_f = state_ssd_fwd[:, j].reshape(n_s, SSD_GROUPS, gw, D_STATE)
            s0_b = state_ssd_bwd[:, j].reshape(n_s, SSD_GROUPS, gw, D_STATE)
            y_b, s_b = _ssd_scan(geom, 1, xbc, decays, s0_b)
            y, s_f = _ssd_scan(geom, 0, xbc, decays, s0_f, y_b, z, dsk, ng)
            new_f.append(s_f.reshape(n_p, heads, SSD_HEAD_DIM, D_STATE))
            new_b.append(s_b.reshape(n_p, heads, SSD_HEAD_DIM, D_STATE))
            x = _matmul_gate_res(geom, y, ssd_w_out[j].astype(BF16), 0, x, mods, 2, 512, "ssd_out")
        else:
            w_qkv = attn_w_qkv[j].astype(BF16)
            cos, sin = _rope_tables(len_s)
            kv_w = N_KV_HEADS * ATTN_HEAD_DIM
            past = cache_k.shape[2]
            q, k_c, v_c, k_lat, v_lat, k_ctx, v_ctx = _attn_qkv(
                geom, x, mods, norm_mix[i], w_qkv, cos, sin, attn_q_norm[j], attn_k_norm[j], n_heads, past)
            k_lat, v_lat = _append_cache(k_lat, v_lat, cache_k[:, j].reshape(n_s, past, kv_w),
                                         cache_v[:, j].reshape(n_s, past, kv_w), len_s)
            o = _attention(q, k_c, v_c, k_lat, v_lat, geom, rep)
            new_k.append(k_ctx.reshape(n_p, len_p, N_KV_HEADS, ATTN_HEAD_DIM))
            new_v.append(v_ctx.reshape(n_p, len_p, N_KV_HEADS, ATTN_HEAD_DIM))
            x = _matmul_gate_res(geom, o, attn_w_o[j].astype(BF16), 0, x, mods, 2, 1024, "attn_out")
        hidden = _norm_mod_matmul(geom, x, mods, norm_ffn[i], w_gu_all, i, [0, d_ff], d_ff, 512, 3, "swiglu", BF16,
                                  "ffn_up")
        if i < depth - 1:
            x = _matmul_gate_res(geom, hidden, w_down_all, i, x, mods, 5, 512, "ffn_down")
        else:
            y_prompt, y_sample = _matmul_gate_res_norm(geom, hidden, w_down_all, i, x, mods, 5, final_norm)

    y_prompt = y_prompt.reshape(n_p, len_p, d)
    y_sample = y_sample.reshape(n_s, len_s, d)
    return (y_prompt, y_sample, jnp.stack(new_f, axis=1), jnp.stack(new_b, axis=1),
            jnp.stack(new_k, axis=1), jnp.stack(new_v, axis=1))
```

```python
import functools
import math

import jax
import jax.numpy as jnp
from jax import lax
from jax.experimental import pallas as pl
from jax.experimental.pallas import tpu as pltpu

F32 = jnp.float32
BF16 = jnp.bfloat16

EPS = 1e-6
GRID_W = 64
SSD_HEAD_DIM = 64
SSD_GROUPS = 8
D_STATE = 128
D_CONV = 5
CHUNK = 128
ATTN_HEAD_DIM = 128
N_KV_HEADS = 4
ROPE_THETA = 10000.0

VMEM_LIMIT_BYTES = 56 * 1024 * 1024
MAX_TOKEN_TILE = 1024
SCAN_ROWS = 2 * CHUNK
CONV_ROWS = 256
CONV_HALO = 16
ATTN_LAT_Q_ROWS = 512
ATTN_CHAIN_ROWS = 512
ATTN_KEY_BLOCK = 256
SCAN_GROUPS_PER_STEP = 8
FINAL_ROWS = 256
PROLOGUE_ROWS = 256
QKV_EPILOGUE_ROWS = 128
LOG2E = math.log2(math.e)


def _params(n_axes):
    return pltpu.CompilerParams(dimension_semantics=("arbitrary",) * n_axes,
                                vmem_limit_bytes=VMEM_LIMIT_BYTES)


def _silu(v):
    return v * jax.nn.sigmoid(v)


def _softplus(v):
    return jnp.maximum(v, 0.0) + jnp.log1p(jnp.exp(-jnp.abs(v)))


def _dot(a, b):
    return jnp.dot(a, b, preferred_element_type=F32)


def _split2(v):
    hi = v.astype(BF16)
    lo = (v - hi.astype(F32)).astype(BF16)
    return hi, lo


def _split3(v):
    hi = v.astype(BF16)
    r = v - hi.astype(F32)
    mid = r.astype(BF16)
    lo = (r - mid.astype(F32)).astype(BF16)
    return hi, mid, lo


def _adaln_kernel(c_ref, w_ref, b_ref, o_ref):
    s = _silu(c_ref[...]).astype(BF16)
    o_ref[0] = _dot(s, w_ref[0].astype(BF16)) + b_ref[0]


def _adaln(cond, w_mod, b_mod):
    depth, d, n = w_mod.shape
    rows = cond.shape[0]
    tn = 1536
    assert n % tn == 0
    return pl.pallas_call(
        _adaln_kernel,
        grid=(depth, n // tn),
        in_specs=[pl.BlockSpec((rows, d), lambda l, j: (0, 0)),
                  pl.BlockSpec((1, d, tn), lambda l, j: (l, 0, j)),
                  pl.BlockSpec((1, 1, tn), lambda l, j: (l, 0, j))],
        out_specs=pl.BlockSpec((1, rows, tn), lambda l, j: (l, 0, j)),
        out_shape=jax.ShapeDtypeStruct((depth, rows, n), F32),
        compiler_params=_params(2),
        name="adaln",
    )(cond, w_mod, b_mod.reshape(depth, 1, n))


class _Geom:
    def __init__(self, n_p, len_p, n_s, len_s):
        self.n_p, self.len_p, self.n_s, self.len_s = n_p, len_p, n_s, len_s
        self.t_p = n_p * len_p
        self.t_s = n_s * len_s
        self.t = self.t_p + self.t_s
        self.tm = math.gcd(math.gcd(self.t_p, len_s), MAX_TOKEN_TILE)

    def mod_row(self, i, tile):
        npt = self.t_p // tile
        per_seq = self.len_s // tile
        return jnp.where(i < npt, 0, 1 + (i - npt) // per_seq)


def _row_blocks(n_rows, block):
    return [slice(r0, r0 + block) for r0 in range(0, n_rows, block)]


def _norm_mod_rows(x_ref, mod_ref, g_ref, rows, shift_row):
    x = x_ref[rows, :]
    y = x * lax.rsqrt(jnp.mean(x * x, axis=-1, keepdims=True) + EPS) * g_ref[...]
    scale1 = 1.0 + mod_ref[0, shift_row + 1:shift_row + 2, :]
    return (y * scale1 + mod_ref[0, shift_row:shift_row + 1, :]).astype(BF16)


def _ssd_in_kernel(mod_ref, g_ref, w_ref, wdt_ref, *rest, z_blocks, sub_rows, n_p_tiles):
    x_refs = rest[:-4]
    z_ref, xbc_ref, dt_ref, h_ref = rest[-4:]
    j = pl.program_id(1)
    first = j == 0

    def first_step(x_ref):
        for rows in _row_blocks(z_ref.shape[0], sub_rows):
            h = _norm_mod_rows(x_ref, mod_ref, g_ref, rows, 0)
            h_ref[rows, :] = h
            z_ref[rows, :] = _dot(h, w_ref[...]).astype(z_ref.dtype)

    if len(x_refs) == 1:
        pl.when(first)(lambda: first_step(x_refs[0]))
    else:
        is_p = pl.program_id(0) < n_p_tiles
        pl.when(jnp.logical_and(first, is_p))(lambda: first_step(x_refs[0]))
        pl.when(jnp.logical_and(first, jnp.logical_not(is_p)))(lambda: first_step(x_refs[1]))

    @pl.when(jnp.logical_not(first))
    def _():
        h = h_ref[...]
        acc = _dot(h, w_ref[...])

        @pl.when(j < z_blocks)
        def _():
            z_ref[...] = acc.astype(z_ref.dtype)

        @pl.when(j >= z_blocks)
        def _():
            xbc_ref[...] = acc.astype(xbc_ref.dtype)

        @pl.when(j == pl.num_programs(1) - 1)
        def _():
            dt_ref[...] = _dot(h, wdt_ref[...])


def _ssd_in_proj(geom, x, mods, gain, w, d_inner, conv_dim, n_dt, tn):
    t, d = geom.t, w.shape[0]
    tm = geom.tm
    assert d_inner % tn == 0 and conv_dim % tn == 0 and (d_inner + conv_dim) % n_dt == 0
    z_blocks, xbc_blocks = d_inner // tn, conv_dim // tn
    dt_block = (d_inner + conv_dim) // n_dt
    xs, x_specs = _residual_specs(geom, x, (tm, d), lambda j: 0, single_buffer=True)
    return pl.pallas_call(
        functools.partial(_ssd_in_kernel, z_blocks=z_blocks, sub_rows=min(tm, PROLOGUE_ROWS),
                          n_p_tiles=geom.t_p // tm),
        grid=(t // tm, z_blocks + xbc_blocks),
        in_specs=[pl.BlockSpec((1, 6, d), lambda i, j: (geom.mod_row(i, tm), 0, 0)),
                  pl.BlockSpec((1, d), lambda i, j: (0, 0)),
                  pl.BlockSpec((d, tn), lambda i, j: (0, j)),
                  pl.BlockSpec((d, n_dt), lambda i, j: (0, dt_block))] + x_specs,
        out_specs=[pl.BlockSpec((tm, tn), lambda i, j: (i, jnp.minimum(j, z_blocks - 1))),
                   pl.BlockSpec((tm, tn), lambda i, j: (i, jnp.maximum(j - z_blocks, 0))),
                   pl.BlockSpec((tm, n_dt), lambda i, j: (i, 0))],
        out_shape=[jax.ShapeDtypeStruct((t, d_inner), BF16),
                   jax.ShapeDtypeStruct((t, conv_dim), BF16),
                   jax.ShapeDtypeStruct((t, n_dt), F32)],
        scratch_shapes=[pltpu.VMEM((tm, d), BF16)],
        compiler_params=_params(2),
        name="ssd_in",
    )(mods, gain.reshape(1, d), w, w, *xs)


def _weight_spec(w, layer, k, tn, col_block0):
    if w.ndim == 2:
        return pl.BlockSpec((k, tn), lambda i, j: (0, col_block0 + j))
    return pl.BlockSpec((None, k, tn), lambda i, j: (layer, 0, col_block0 + j))


def _nm_kernel(x_ref, mod_ref, g_ref, *rest, n_w, shift_row, act, sub_rows):
    w_refs = rest[:n_w]
    o_ref = rest[n_w]
    h_ref = rest[n_w + 1]

    def emit(rows, h):
        if act == "swiglu":
            gate = _dot(h, w_refs[0][...])
            up = _dot(h, w_refs[1][...])
            o_ref[rows, :] = (_silu(gate) * up).astype(o_ref.dtype)
        else:
            o_ref[rows, :] = _dot(h, w_refs[0][...]).astype(o_ref.dtype)

    first = pl.program_id(1) == 0

    @pl.when(first)
    def _():
        for rows in _row_blocks(o_ref.shape[0], sub_rows):
            h = _norm_mod_rows(x_ref, mod_ref, g_ref, rows, shift_row)
            h_ref[rows, :] = h
            emit(rows, h)

    @pl.when(jnp.logical_not(first))
    def _():
        emit(slice(None), h_ref[...])


def _norm_mod_matmul(geom, x, mods, gain, w, layer, col_offsets, n_cols, tn, shift_row, act, out_dtype, name):
    t, d = x.shape
    tm = geom.tm
    assert n_cols % tn == 0 and all(off % tn == 0 for off in col_offsets)
    w_specs = [_weight_spec(w, layer, d, tn, off // tn) for off in col_offsets]
    return pl.pallas_call(
        functools.partial(_nm_kernel, n_w=len(col_offsets), shift_row=shift_row, act=act,
                          sub_rows=min(tm, PROLOGUE_ROWS)),
        grid=(t // tm, n_cols // tn),
        in_specs=[pl.BlockSpec((tm, d), lambda i, j: (i, 0)),
                  pl.BlockSpec((1, 6, d), lambda i, j: (geom.mod_row(i, tm), 0, 0)),
                  pl.BlockSpec((1, d), lambda i, j: (0, 0))] + w_specs,
        out_specs=pl.BlockSpec((tm, tn), lambda i, j: (i, j)),
        out_shape=jax.ShapeDtypeStruct((t, n_cols), out_dtype),
        scratch_shapes=[pltpu.VMEM((tm, d), BF16)],
        compiler_params=_params(2),
        name=name,
    )(x, mods, gain.reshape(1, d), *([w] * len(col_offsets)))


def _gate_res_kernel(a_ref, w_ref, mod_ref, *rest, gate_row, n_p_tiles):
    x_refs, o_ref = rest[:-1], rest[-1]
    gated = mod_ref[0, gate_row:gate_row + 1, :] * _dot(a_ref[...], w_ref[...])
    if len(x_refs) == 1:
        o_ref[...] = x_refs[0][...] + gated
    else:
        is_p = pl.program_id(0) < n_p_tiles

        @pl.when(is_p)
        def _():
            o_ref[...] = x_refs[0][...] + gated

        @pl.when(jnp.logical_not(is_p))
        def _():
            o_ref[...] = x_refs[1][...] + gated


def _residual_specs(geom, x, block, col_of, single_buffer=False):
    tm = geom.tm
    npt = geom.t_p // tm
    if not isinstance(x, tuple):
        return [x], [pl.BlockSpec(block, lambda i, j: (i, col_of(j)))]
    mode = dict(pipeline_mode=pl.Buffered(1)) if single_buffer else {}
    return list(x), [pl.BlockSpec(block, lambda i, j: (jnp.minimum(i, npt - 1), col_of(j)), **mode),
                     pl.BlockSpec(block, lambda i, j: (jnp.maximum(i - npt, 0), col_of(j)))]


def _matmul_gate_res(geom, a, w, layer, x, mods, gate_row, tn, name):
    t, k = a.shape
    d = w.shape[-1]
    tm = geom.tm
    assert d % tn == 0
    xs, x_specs = _residual_specs(geom, x, (tm, tn), lambda j: j)
    return pl.pallas_call(
        functools.partial(_gate_res_kernel, gate_row=gate_row, n_p_tiles=geom.t_p // tm),
        grid=(t // tm, d // tn),
        in_specs=[pl.BlockSpec((tm, k), lambda i, j: (i, 0)),
                  _weight_spec(w, layer, k, tn, 0),
                  pl.BlockSpec((1, 6, tn), lambda i, j: (geom.mod_row(i, tm), 0, j))] + x_specs,
        out_specs=pl.BlockSpec((tm, tn), lambda i, j: (i, j)),
        out_shape=jax.ShapeDtypeStruct((t, d), F32),
        compiler_params=_params(2),
        name=name,
    )(a, w, mods, *xs)


def _gate_res_norm_kernel(a_ref, w_ref, x_ref, mod_ref, g_ref, yp_ref, ys_ref, *, gate_row, n_p_tiles):
    x = x_ref[...] + mod_ref[0, gate_row:gate_row + 1, :] * _dot(a_ref[...], w_ref[...])
    y = x * lax.rsqrt(jnp.mean(x * x, axis=-1, keepdims=True) + EPS) * g_ref[...]
    is_p = pl.program_id(0) < n_p_tiles

    @pl.when(is_p)
    def _():
        yp_ref[...] = y

    @pl.when(jnp.logical_not(is_p))
    def _():
        ys_ref[...] = y


def _matmul_gate_res_norm(geom, a, w, layer, x, mods, gate_row, gain):
    t, k = a.shape
    d = w.shape[-1]
    tm = math.gcd(math.gcd(geom.t_p, geom.len_s), FINAL_ROWS)
    npt = geom.t_p // tm
    w_block = (k, d) if w.ndim == 2 else (None, k, d)
    w_index = (lambda i: (0, 0)) if w.ndim == 2 else (lambda i: (layer, 0, 0))
    return pl.pallas_call(
        functools.partial(_gate_res_norm_kernel, gate_row=gate_row, n_p_tiles=npt),
        grid=(t // tm,),
        in_specs=[pl.BlockSpec((tm, k), lambda i: (i, 0)),
                  pl.BlockSpec(w_block, w_index, pipeline_mode=pl.Buffered(1)),
                  pl.BlockSpec((tm, d), lambda i: (i, 0)),
                  pl.BlockSpec((1, 6, d), lambda i: (geom.mod_row(i, tm), 0, 0)),
                  pl.BlockSpec((1, d), lambda i: (0, 0))],
        out_specs=[pl.BlockSpec((tm, d), lambda i: (jnp.minimum(i, npt - 1), 0)),
                   pl.BlockSpec((tm, d), lambda i: (jnp.maximum(i - npt, 0), 0))],
        out_shape=[jax.ShapeDtypeStruct((geom.t_p, d), F32),
                   jax.ShapeDtypeStruct((geom.t_s, d), F32)],
        compiler_params=_params(1),
        name="ffn_down_final",
    )(a, w, x, mods, gain.reshape(1, d))


def _conv_kernel(u_ref, prev_ref, next_ref, w_ref, b_ref, o_ref, ext_ref, *, n_p_blocks, p_blocks, s_blocks):
    i = pl.program_id(0)
    lb = u_ref.shape[0]
    is_p = i < n_p_blocks
    pos = jnp.where(is_p, i % p_blocks, (i - n_p_blocks) % s_blocks)
    last = jnp.where(is_p, p_blocks, s_blocks) - 1
    ext_ref[0:CONV_HALO, :] = jnp.where(pos > 0, prev_ref[...].astype(F32), 0.0)
    ext_ref[CONV_HALO:CONV_HALO + lb, :] = u_ref[...].astype(F32)
    ext_ref[CONV_HALO + lb:2 * CONV_HALO + lb, :] = jnp.where(pos < last, next_ref[...].astype(F32), 0.0)
    ext = ext_ref[...]
    n_ext = ext.shape[0]
    centre = D_CONV // 2
    acc = b_ref[...] + w_ref[centre:centre + 1, :] * ext[CONV_HALO:CONV_HALO + lb, :]
    for k in range(D_CONV):
        if k != centre:
            shifted = pltpu.roll(ext, (centre - k) % n_ext, 0)
            acc = acc + w_ref[k:k + 1, :] * shifted[CONV_HALO:CONV_HALO + lb, :]
    o_ref[...] = _silu(acc).astype(o_ref.dtype)


def _ssd_conv(geom, xbc, conv_w, conv_b):
    t, c = xbc.shape
    lb, ct = CONV_ROWS, 2048
    assert geom.len_p % lb == 0 and geom.len_s % lb == 0 and c % ct == 0
    hb = lb // CONV_HALO
    n_halo_blocks = t // CONV_HALO
    kern = functools.partial(_conv_kernel, n_p_blocks=geom.t_p // lb, p_blocks=geom.len_p // lb,
                             s_blocks=geom.len_s // lb)
    return pl.pallas_call(
        kern,
        grid=(t // lb, c // ct),
        in_specs=[pl.BlockSpec((lb, ct), lambda i, j: (i, j)),
                  pl.BlockSpec((CONV_HALO, ct), lambda i, j: (jnp.maximum(i * hb - 1, 0), j)),
                  pl.BlockSpec((CONV_HALO, ct), lambda i, j: (jnp.minimum((i + 1) * hb, n_halo_blocks - 1), j)),
                  pl.BlockSpec((D_CONV, ct), lambda i, j: (0, j)),
                  pl.BlockSpec((1, ct), lambda i, j: (0, j))],
        out_specs=pl.BlockSpec((lb, ct), lambda i, j: (i, j)),
        out_shape=jax.ShapeDtypeStruct((t, c), BF16),
        scratch_shapes=[pltpu.VMEM((lb + 2 * CONV_HALO, ct), F32)],
        compiler_params=_params(2),
        name="ssd_conv",
    )(xbc, xbc, xbc, conv_w, conv_b.reshape(1, c))


def _dt_prep_kernel(dt_ref, alog_ref, bias_ref, cs_ref, src_ref, e_ref, w_ref):
    q = CHUNK
    n_dir_heads = dt_ref.shape[1]
    li = lax.broadcasted_iota(jnp.int32, (q, q), 0)
    si = lax.broadcasted_iota(jnp.int32, (q, q), 1)
    lower = (li >= si).astype(BF16)
    upper = (li <= si).astype(BF16)
    fwd = lax.broadcasted_iota(jnp.int32, (1, n_dir_heads), 1) < n_dir_heads // 2
    a_scale = -jnp.exp(alog_ref[...]) * LOG2E
    for r0 in range(0, dt_ref.shape[0], q):
        dt = _softplus(dt_ref[r0:r0 + q, :] + bias_ref[...])
        parts = _split3(dt * a_scale)
        cs = jnp.where(fwd, sum(_dot(lower, p) for p in parts), sum(_dot(upper, p) for p in parts))
        total = jnp.where(fwd, cs[q - 1:q, :], cs[0:1, :])
        cs_ref[:, r0:r0 + q] = cs.T
        src_ref[:, r0:r0 + q] = (cs - jnp.log2(dt)).T
        e_ref[:, r0:r0 + q] = jnp.exp2(cs).T
        w_ref[:, r0:r0 + q] = (dt * jnp.exp2(total - cs)).T


def _dt_prep(dt_raw, a_log, dt_bias):
    t, nh = dt_raw.shape
    rb = 1024 if t % 1024 == 0 else SCAN_ROWS
    out = jax.ShapeDtypeStruct((nh, t), F32)
    head_major = pl.BlockSpec((nh, rb), lambda i: (0, i))
    return pl.pallas_call(
        _dt_prep_kernel,
        grid=(t // rb,),
        in_specs=[pl.BlockSpec((rb, nh), lambda i: (i, 0)),
                  pl.BlockSpec((1, nh), lambda i: (0, 0)),
                  pl.BlockSpec((1, nh), lambda i: (0, 0))],
        out_specs=[head_major] * 4,
        out_shape=[out] * 4,
        compiler_params=_params(1),
        name="ssd_dt_prep",
    )(dt_raw, a_log.reshape(1, nh), dt_bias.reshape(1, nh))


def _scan_kernel(*refs, rev, fuse_norm, n_steps, n_p_steps, p_steps, s_steps, hpg, gps):
    if fuse_norm:
        (x_ref, b_ref, c_ref, cs_ref, src_ref, e_ref, w_ref, s0_ref, yo_ref, z_ref, dsk_ref, ng_ref,
         y_ref, sfin_ref, st_ref) = refs
    else:
        (x_ref, b_ref, c_ref, cs_ref, src_ref, e_ref, w_ref, s0_ref,
         y_ref, sfin_ref, st_ref) = refs
    q = CHUNK
    hd = SSD_HEAD_DIM
    n = D_STATE
    width = hpg * hd
    step = pl.program_id(1)
    blk = (n_steps - 1 - step) if rev else step
    is_p = blk < n_p_steps
    pos = jnp.where(is_p, blk % p_steps, (blk - n_p_steps) % s_steps)
    last = jnp.where(is_p, p_steps, s_steps) - 1
    seq_start = (pos == last) if rev else (pos == 0)
    seq_end = (pos == 0) if rev else (pos == last)

    @pl.when(jnp.logical_and(seq_start, is_p))
    def _():
        st_ref[...] = jnp.zeros_like(st_ref)

    @pl.when(jnp.logical_and(seq_start, jnp.logical_not(is_p)))
    def _():
        for gi in range(gps):
            st_ref[gi] = s0_ref[gi].T

    li = lax.broadcasted_iota(jnp.int32, (q, q), 0)
    si = lax.broadcasted_iota(jnp.int32, (q, q), 1)
    keep = (li <= si) if rev else (li >= si)
    expand = (lax.broadcasted_iota(jnp.int32, (2 * hpg, width), 1) // hd
              == lax.broadcasted_iota(jnp.int32, (2 * hpg, width), 0) % hpg).astype(BF16)
    lane = lax.broadcasted_iota(jnp.int32, (q, 2 * hd), 1)
    edge = 0 if rev else q - 1

    def widen(v_r):
        hi, lo = _split2(v_r)
        return lax.dot_general(jnp.concatenate([hi, lo], axis=0), expand, (((0,), (0,)), ((), ())),
                               preferred_element_type=F32)

    def chunk(gi, r0):
        cols = slice(gi * width, (gi + 1) * width)
        heads = slice(gi * hpg, (gi + 1) * hpg)
        x = x_ref[r0:r0 + q, cols]
        bm = b_ref[r0:r0 + q, gi * n:(gi + 1) * n]
        cm = c_ref[r0:r0 + q, gi * n:(gi + 1) * n]
        cs_c = cs_ref[heads, r0:r0 + q].T
        src_r = src_ref[heads, r0:r0 + q]
        e_x = widen(e_ref[heads, r0:r0 + q])
        w_x = widen(w_ref[heads, r0:r0 + q])

        cb = lax.dot_general(cm, bm, (((1,), (1,)), ((), ())), preferred_element_type=F32)
        state = st_ref[gi]
        y = _dot(cm, state.astype(BF16)) * e_x
        parts = []
        for p in range(hpg // 2):
            ms = []
            for r in (2 * p, 2 * p + 1):
                seg = cs_c[:, r:r + 1] - src_r[r:r + 1, :]
                ms.append((cb * jnp.exp2(jnp.where(keep, seg, -jnp.inf))).astype(BF16))
            x2 = x[:, 2 * p * hd:2 * (p + 1) * hd]
            zero = jnp.zeros_like(x2)
            rhs = jnp.concatenate([jnp.where(lane < hd, x2, zero), jnp.where(lane >= hd, x2, zero)], axis=0)
            parts.append(_dot(jnp.concatenate(ms, axis=1), rhs))
        y = y + jnp.concatenate(parts, axis=1)

        xf = x.astype(F32)
        xw = (xf * w_x).astype(BF16)
        st_ref[gi] = state * e_x[edge:edge + 1, :] + lax.dot_general(
            bm, xw, (((0,), (0,)), ((), ())), preferred_element_type=F32)

        if fuse_norm:
            y = y + yo_ref[r0:r0 + q, cols].astype(F32) + dsk_ref[gi] * xf
            yz = y * _silu(z_ref[r0:r0 + q, cols].astype(F32))
            yz = yz * lax.rsqrt(jnp.mean(yz * yz, axis=-1, keepdims=True) + EPS)
            y = yz * ng_ref[gi]
        y_ref[r0:r0 + q, cols] = y.astype(y_ref.dtype)

    n_chunks = x_ref.shape[0] // q
    for ci in (range(n_chunks - 1, -1, -1) if rev else range(n_chunks)):
        for gi in range(gps):
            chunk(gi, ci * q)

    @pl.when(jnp.logical_and(seq_end, is_p))
    def _():
        for gi in range(gps):
            sfin_ref[gi] = st_ref[gi].T


def _ssd_scan(geom, direction, xbc, decays, s0, y_other=None, z=None, dsk=None, ng=None):
    t = xbc.shape[0]
    g_n = SSD_GROUPS
    gps = SCAN_GROUPS_PER_STEP
    hpg = decays[0].shape[0] // (2 * g_n)
    width = hpg * SSD_HEAD_DIM
    n = D_STATE
    rb = SCAN_ROWS
    assert geom.len_p % rb == 0 and geom.len_s % rb == 0 and g_n % gps == 0
    n_steps = t // rb
    n_p_steps, p_steps, s_steps = geom.t_p // rb, geom.len_p // rb, geom.len_s // rb
    rev = direction == 1
    fuse = y_other is not None
    x_cols = g_n * width
    b_blk0 = x_cols // (gps * n)
    c_blk0 = (x_cols + g_n * n) // (gps * n)
    head_blk0 = direction * (g_n // gps)

    def blk(s):
        return (n_steps - 1 - s) if rev else s

    def s0_idx(g, s):
        return (jnp.clip((blk(s) - n_p_steps) // s_steps, 0, geom.n_s - 1), g, 0, 0)

    def sfin_idx(g, s):
        return (jnp.minimum(blk(s) // p_steps, geom.n_p - 1), g, 0, 0)

    wide = pl.BlockSpec((rb, gps * width), lambda g, s: (blk(s), g))
    head_major = pl.BlockSpec((gps * hpg, rb), lambda g, s: (head_blk0 + g, blk(s)))
    in_specs = [wide,
                pl.BlockSpec((rb, gps * n), lambda g, s: (blk(s), b_blk0 + g)),
                pl.BlockSpec((rb, gps * n), lambda g, s: (blk(s), c_blk0 + g)),
                head_major, head_major, head_major, head_major,
                pl.BlockSpec((None, gps, width, n), s0_idx)]
    args = [xbc, xbc, xbc, *decays, s0]
    if fuse:
        in_specs += [wide, wide,
                     pl.BlockSpec((gps, 1, width), lambda g, s: (g, 0, 0)),
                     pl.BlockSpec((gps, 1, width), lambda g, s: (g, 0, 0))]
        args += [y_other, z, dsk, ng]
    kern = functools.partial(_scan_kernel, rev=rev, fuse_norm=fuse, n_steps=n_steps, n_p_steps=n_p_steps,
                             p_steps=p_steps, s_steps=s_steps, hpg=hpg, gps=gps)
    return pl.pallas_call(
        kern,
        grid=(g_n // gps, n_steps),
        in_specs=in_specs,
        out_specs=[wide, pl.BlockSpec((None, gps, width, n), sfin_idx)],
        out_shape=[jax.ShapeDtypeStruct((t, x_cols), BF16),
                   jax.ShapeDtypeStruct((geom.n_p, g_n, width, n), F32)],
        scratch_shapes=[pltpu.VMEM((gps, n, width), F32)],
        compiler_params=_params(2),
        name="ssd_scan_bwd" if rev else "ssd_scan_fwd",
    )(*args)


def _attn_qkv_kernel(x_ref, mod_ref, g_ref, w_ref, cos_ref, sin_ref, qg_ref, kg_ref,
                     q_ref, kc_ref, vc_ref, kl_ref, vl_ref, kf_ref, vf_ref, h_ref,
                     *, n_p_tiles, q_blocks, sub_rows):
    i = pl.program_id(0)
    j = pl.program_id(1)
    hd = ATTN_HEAD_DIM
    is_p = i < n_p_tiles
    is_s = jnp.logical_not(is_p)
    first = j == 0
    tm, tn = q_ref.shape
    heads = tn // hd
    blocks = _row_blocks(tm, sub_rows)
    lane = lax.broadcasted_iota(jnp.int32, (sub_rows, hd), 1)
    first_half = (lane % (hd // 2)) < (hd // 4)

    def norm(v, gain):
        return v * lax.rsqrt(jnp.mean(v * v, axis=-1, keepdims=True) + EPS) * gain

    def rope(v, c, s):
        partner = jnp.where(first_half, pltpu.roll(v, hd - hd // 4, 1), pltpu.roll(v, hd // 4, 1))
        return v * c + partner * s

    def q_rows(rows, h):
        q_scale = hd ** -0.5 * LOG2E
        acc = _dot(h, w_ref[...])
        cos_q = jnp.where(is_p, 1.0, cos_ref[rows, :]) * q_scale
        sin_q = jnp.where(is_p, 0.0, sin_ref[rows, :]) * q_scale
        for hh in range(heads):
            cols = slice(hh * hd, (hh + 1) * hd)
            q_ref[rows, cols] = rope(norm(acc[:, cols], qg_ref[...]), cos_q, sin_q).astype(q_ref.dtype)

    @pl.when(first)
    def _():
        for rows in blocks:
            h = _norm_mod_rows(x_ref, mod_ref, g_ref, rows, 0)
            h_ref[rows, :] = h
            q_rows(rows, h)

    @pl.when(jnp.logical_and(jnp.logical_not(first), j < q_blocks))
    def _():
        for rows in blocks:
            q_rows(rows, h_ref[rows, :])

    @pl.when(jnp.logical_and(j == q_blocks, is_p))
    def _():
        for rows in blocks:
            acc = _dot(h_ref[rows, :], w_ref[...])
            for hh in range(heads):
                cols = slice(hh * hd, (hh + 1) * hd)
                normed = norm(acc[:, cols], kg_ref[...])
                kf_ref[rows, cols] = normed
                kc_ref[rows, cols] = normed.astype(kc_ref.dtype)

    @pl.when(jnp.logical_and(j == q_blocks, is_s))
    def _():
        for rows in blocks:
            acc = _dot(h_ref[rows, :], w_ref[...])
            for hh in range(heads):
                cols = slice(hh * hd, (hh + 1) * hd)
                normed = norm(acc[:, cols], kg_ref[...])
                kl_ref[rows, cols] = rope(normed, cos_ref[rows, :], sin_ref[rows, :]).astype(kl_ref.dtype)

    @pl.when(jnp.logical_and(j == q_blocks + 1, is_p))
    def _():
        acc = _dot(h_ref[...], w_ref[...])
        vf_ref[...] = acc
        vc_ref[...] = acc.astype(vc_ref.dtype)

    @pl.when(jnp.logical_and(j == q_blocks + 1, is_s))
    def _():
        vl_ref[...] = _dot(h_ref[...], w_ref[...]).astype(vl_ref.dtype)


def _attn_qkv(geom, x, mods, gain, w, cos, sin, q_gain, k_gain, n_heads, past):
    t, d = x.shape
    tm = geom.tm
    hd, n_kv = ATTN_HEAD_DIM, N_KV_HEADS
    tn = n_kv * hd
    q_cols = n_heads * hd
    assert q_cols % tn == 0 and q_cols >= tn and w.shape[1] == q_cols + 2 * tn
    q_blocks = q_cols // tn
    npt = geom.t_p // tm
    per_seq = geom.len_s // tm

    def tab_idx(i, j):
        return (jnp.where(i < npt, 0, (i - npt) % per_seq), 0)

    def ctx_idx(i, j):
        return (jnp.minimum(i, npt - 1), 0)

    def lat_idx(i, j):
        return (jnp.clip((i - npt) // per_seq, 0, geom.n_s - 1), jnp.where(i < npt, 0, (i - npt) % per_seq), 0)

    kv_lat = jax.ShapeDtypeStruct((geom.n_s, geom.len_s + past, tn), BF16)
    return pl.pallas_call(
        functools.partial(_attn_qkv_kernel, n_p_tiles=npt, q_blocks=q_blocks,
                          sub_rows=min(tm, QKV_EPILOGUE_ROWS)),
        grid=(t // tm, q_blocks + 2),
        in_specs=[pl.BlockSpec((tm, d), lambda i, j: (i, 0)),
                  pl.BlockSpec((1, 6, d), lambda i, j: (geom.mod_row(i, tm), 0, 0)),
                  pl.BlockSpec((1, d), lambda i, j: (0, 0)),
                  pl.BlockSpec((d, tn), lambda i, j: (0, j)),
                  pl.BlockSpec((tm, hd), tab_idx),
                  pl.BlockSpec((tm, hd), tab_idx),
                  pl.BlockSpec((1, hd), lambda i, j: (0, 0)),
                  pl.BlockSpec((1, hd), lambda i, j: (0, 0))],
        out_specs=[pl.BlockSpec((tm, tn), lambda i, j: (i, jnp.minimum(j, q_blocks - 1))),
                   pl.BlockSpec((tm, tn), ctx_idx),
                   pl.BlockSpec((tm, tn), ctx_idx),
                   pl.BlockSpec((None, tm, tn), lat_idx),
                   pl.BlockSpec((None, tm, tn), lat_idx),
                   pl.BlockSpec((tm, tn), ctx_idx),
                   pl.BlockSpec((tm, tn), ctx_idx)],
        out_shape=[jax.ShapeDtypeStruct((t, q_cols), BF16),
                   jax.ShapeDtypeStruct((geom.t_p, tn), BF16),
                   jax.ShapeDtypeStruct((geom.t_p, tn), BF16),
                   kv_lat, kv_lat,
                   jax.ShapeDtypeStruct((geom.t_p, tn), F32),
                   jax.ShapeDtypeStruct((geom.t_p, tn), F32)],
        scratch_shapes=[pltpu.VMEM((tm, d), BF16)],
        compiler_params=_params(2),
        name="attn_qkv",
    )(x, mods, gain.reshape(1, d), w, cos, sin, q_gain.reshape(1, hd), k_gain.reshape(1, hd))


def _append_cache_kernel(ck_ref, cv_ref, k_in, v_in, k_ref, v_ref):
    k_ref[...] = ck_ref[...].astype(k_ref.dtype)
    v_ref[...] = cv_ref[...].astype(v_ref.dtype)


def _append_cache(k_lat, v_lat, cache_k, cache_v, len_s):
    n_s, past, kv = cache_k.shape
    assert len_s % past == 0
    cache_spec = pl.BlockSpec((None, past, kv), lambda b: (b, 0, 0))
    tail_spec = pl.BlockSpec((None, past, kv), lambda b: (b, len_s // past, 0))
    any_spec = pl.BlockSpec(memory_space=pl.ANY)
    return pl.pallas_call(
        _append_cache_kernel,
        grid=(n_s,),
        in_specs=[cache_spec, cache_spec, any_spec, any_spec],
        out_specs=[tail_spec, tail_spec],
        out_shape=[jax.ShapeDtypeStruct(k_lat.shape, k_lat.dtype)] * 2,
        input_output_aliases={2: 0, 3: 1},
        compiler_params=_params(1),
        name="append_cache",
    )(cache_k, cache_v, k_lat, v_lat)


def _attn_ctx_kernel(q_ref, k_ref, v_ref, o_ref, *, rep):
    hd = ATTN_HEAD_DIM
    rows = q_ref.shape[0]
    for kv in range(k_ref.shape[1] // hd):
        k = k_ref[:, kv * hd:(kv + 1) * hd]
        v = v_ref[:, kv * hd:(kv + 1) * hd]
        heads = range(kv * rep, (kv + 1) * rep)
        q = jnp.concatenate([q_ref[:, r * hd:(r + 1) * hd] for r in heads], axis=0)
        s = lax.dot_general(q, k, (((1,), (1,)), ((), ())), preferred_element_type=F32)
        p = jnp.exp2(s - jnp.max(s, axis=-1, keepdims=True))
        denom = jnp.sum(p, axis=-1, keepdims=True)
        o = _dot(p.astype(BF16), v) / denom
        for n, r in enumerate(heads):
            o_ref[:, r * hd:(r + 1) * hd] = o[n * rows:(n + 1) * rows, :].astype(o_ref.dtype)


def _attn_stream_kernel(q_ref, k_ref, v_ref, o_in, o_ref, *, rep, chain_rows, key_block):
    del o_in
    hd = ATTN_HEAD_DIM
    n_keys = k_ref.shape[0]
    for r in range(rep):
        cols = slice(r * hd, (r + 1) * hd)
        for rows in _row_blocks(q_ref.shape[0], chain_rows):
            q = q_ref[rows, cols]
            m = jnp.full((chain_rows, 1), -jnp.inf, F32)
            denom = jnp.zeros((chain_rows, 1), F32)
            acc = jnp.zeros((chain_rows, hd), F32)
            for c0 in range(0, n_keys, key_block):
                keys = slice(c0, c0 + key_block)
                s = lax.dot_general(q, k_ref[keys, :], (((1,), (1,)), ((), ())), preferred_element_type=F32)
                m_new = jnp.maximum(m, jnp.max(s, axis=-1, keepdims=True))
                alpha = jnp.exp2(m - m_new)
                p = jnp.exp2(s - m_new)
                denom = alpha * denom + jnp.sum(p, axis=-1, keepdims=True)
                acc = alpha * acc + _dot(p.astype(BF16), v_ref[keys, :])
                m = m_new
            o_ref[rows, cols] = (acc / denom).astype(o_ref.dtype)


def _attention(q, k_ctx, v_ctx, k_lat, v_lat, geom, rep):
    t, qd = q.shape
    hd, n_kv = ATTN_HEAD_DIM, N_KV_HEADS
    out_shape = jax.ShapeDtypeStruct((t, qd), BF16)
    o = pl.pallas_call(
        functools.partial(_attn_ctx_kernel, rep=rep),
        grid=(geom.n_p,),
        in_specs=[pl.BlockSpec((geom.len_p, qd), lambda b: (b, 0)),
                  pl.BlockSpec((geom.len_p, n_kv * hd), lambda b: (b, 0)),
                  pl.BlockSpec((geom.len_p, n_kv * hd), lambda b: (b, 0))],
        out_specs=pl.BlockSpec((geom.len_p, qd), lambda b: (b, 0)),
        out_shape=out_shape,
        compiler_params=_params(1),
        name="attn_ctx",
    )(q, k_ctx, v_ctx)
    qb = math.gcd(math.gcd(geom.len_s, geom.t_p), ATTN_LAT_Q_ROWS)
    blk_q = (qb, rep * hd)
    s_blocks = geom.len_s // qb
    q0 = geom.t_p // qb
    l_all = k_lat.shape[1]
    return pl.pallas_call(
        functools.partial(_attn_stream_kernel, rep=rep, chain_rows=math.gcd(qb, ATTN_CHAIN_ROWS),
                          key_block=math.gcd(l_all, ATTN_KEY_BLOCK)),
        grid=(geom.n_s, n_kv, s_blocks),
        in_specs=[pl.BlockSpec(blk_q, lambda b, h, i: (q0 + b * s_blocks + i, h)),
                  pl.BlockSpec((None, l_all, hd), lambda b, h, i: (b, 0, h)),
                  pl.BlockSpec((None, l_all, hd), lambda b, h, i: (b, 0, h)),
                  pl.BlockSpec(memory_space=pl.ANY)],
        out_specs=pl.BlockSpec(blk_q, lambda b, h, i: (q0 + b * s_blocks + i, h)),
        out_shape=out_shape,
        input_output_aliases={3: 0},
        compiler_params=_params(3),
        name="attn_lat",
    )(q, k_lat, v_lat, o)


def _rope_tables(length):
    half = ATTN_HEAD_DIM // 2
    pos = jnp.arange(length, dtype=jnp.int32)
    row_pos = (pos // GRID_W).astype(F32)
    col_pos = (pos % GRID_W).astype(F32)
    inv = ROPE_THETA ** (-jnp.arange(0, half, 2, dtype=F32) / half)
    ang_r = row_pos[:, None] * inv[None, :]
    ang_c = col_pos[:, None] * inv[None, :]
    cos = jnp.concatenate([jnp.cos(ang_r)] * 2 + [jnp.cos(ang_c)] * 2, axis=-1)
    sin = jnp.concatenate([-jnp.sin(ang_r), jnp.sin(ang_r), -jnp.sin(ang_c), jnp.sin(ang_c)], axis=-1)
    return cos, sin


def kernel(x_prompt, x_sample, state_ssd_fwd, state_ssd_bwd, cache_k, cache_v, c, c_ctx, w_mod, b_mod, norm_mix, norm_ffn, ssd_w_in, ssd_conv_w, ssd_conv_b, ssd_a_log, ssd_dt_bias, ssd_d, ssd_norm, ssd_w_out, attn_w_qkv, attn_q_norm, attn_k_norm, attn_w_o, ffn_w_gu, ffn_w_down, final_norm):
    n_p, len_p, d = x_prompt.shape
    n_s, len_s, _ = x_sample.shape
    depth = w_mod.shape[0]
    geom = _Geom(n_p, len_p, n_s, len_s)
    d_ff = ffn_w_down.shape[1]
    d_inner = ssd_w_out.shape[1]
    heads = d_inner // SSD_HEAD_DIM
    hpg = heads // SSD_GROUPS
    gw = hpg * SSD_HEAD_DIM
    conv_dim = d_inner + 2 * SSD_GROUPS * D_STATE
    n_heads = attn_w_o.shape[1] // ATTN_HEAD_DIM
    rep = n_heads // N_KV_HEADS

    x = (x_prompt.reshape(geom.t_p, d), x_sample.reshape(geom.t_s, d))
    w_gu_all = ffn_w_gu.astype(BF16)
    w_down_all = ffn_w_down.astype(BF16)

    cond_rows = -(-(1 + n_s) // 8) * 8
    cond = jnp.zeros((cond_rows, d), F32).at[0].set(c_ctx).at[1:1 + n_s].set(c)
    mods_all = _adaln(cond, w_mod, b_mod).reshape(depth, cond_rows, 6, d)

    new_f, new_b, new_k, new_v = [], [], [], []
    for i in range(depth):
        mods = mods_all[i]
        j = i // 2
        if i % 2 == 0:
            w_in = ssd_w_in[j].astype(BF16)
            z, xbc, dt_raw = _ssd_in_proj(geom, x, mods, norm_mix[i], w_in, d_inner, conv_dim, 2 * heads, 1024)
            xbc = _ssd_conv(geom, xbc, ssd_conv_w[j], ssd_conv_b[j])
            decays = _dt_prep(dt_raw, ssd_a_log[j], ssd_dt_bias[j])
            dsk = jnp.repeat(ssd_d[j, 0] + ssd_d[j, 1], SSD_HEAD_DIM).reshape(SSD_GROUPS, 1, gw)
            ng = ssd_norm[j].reshape(SSD_GROUPS, 1, gw)
            s0_f = state_ssd_fwd[:, j].reshape(n_s, SSD_GROUPS, gw, D_STATE)
            s0_b = state_ssd_bwd[:, j].reshape(n_s, SSD_GROUPS, gw, D_STATE)
            y_b, s_b = _ssd_scan(geom, 1, xbc, decays, s0_b)
            y, s_f = _ssd_scan(geom, 0, xbc, decays, s0_f, y_b, z, dsk, ng)
            new_f.append(s_f.reshape(n_p, heads, SSD_HEAD_DIM, D_STATE))
            new_b.append(s_b.reshape(n_p, heads, SSD_HEAD_DIM, D_STATE))
            x = _matmul_gate_res(geom, y, ssd_w_out[j].astype(BF16), 0, x, mods, 2, 512, "ssd_out")
        else:
            w_qkv = attn_w_qkv[j].astype(BF16)
            cos, sin = _rope_tables(len_s)
            kv_w = N_KV_HEADS * ATTN_HEAD_DIM
            past = cache_k.shape[2]
            q, k_c, v_c, k_lat, v_lat, k_ctx, v_ctx = _attn_qkv(
                geom, x, mods, norm_mix[i], w_qkv, cos, sin, attn_q_norm[j], attn_k_norm[j], n_heads, past)
            k_lat, v_lat = _append_cache(k_lat, v_lat, cache_k[:, j].reshape(n_s, past, kv_w),
                                         cache_v[:, j].reshape(n_s, past, kv_w), len_s)
            o = _attention(q, k_c, v_c, k_lat, v_lat, geom, rep)
            new_k.append(k_ctx.reshape(n_p, len_p, N_KV_HEADS, ATTN_HEAD_DIM))
            new_v.append(v_ctx.reshape(n_p, len_p, N_KV_HEADS, ATTN_HEAD_DIM))
            x = _matmul_gate_res(geom, o, attn_w_o[j].astype(BF16), 0, x, mods, 2, 1024, "attn_out")
        hidden = _norm_mod_matmul(geom, x, mods, norm_ffn[i], w_gu_all, i, [0, d_ff], d_ff, 512, 3, "swiglu", BF16,
                                  "ffn_up")
        if i < depth - 1:
            x = _matmul_gate_res(geom, hidden, w_down_all, i, x, mods, 5, 512, "ffn_down")
        else:
            y_prompt, y_sample = _matmul_gate_res_norm(geom, hidden, w_down_all, i, x, mods, 5, final_norm)

    y_prompt = y_prompt.reshape(n_p, len_p, d)
    y_sample = y_sample.reshape(n_s, len_s, d)
    return (y_prompt, y_sample, jnp.stack(new_f, axis=1), jnp.stack(new_b, axis=1),
            jnp.stack(new_k, axis=1), jnp.stack(new_v, axis=1))
```

```python
import functools
import math

import jax
import jax.numpy as jnp
from jax import lax
from jax.experimental import pallas as pl
from jax.experimental.pallas import tpu as pltpu

F32 = jnp.float32
BF16 = jnp.bfloat16

EPS = 1e-6
GRID_W = 64
SSD_HEAD_DIM = 64
SSD_GROUPS = 8
D_STATE = 128
D_CONV = 5
CHUNK = 128
ATTN_HEAD_DIM = 128
N_KV_HEADS = 4
ROPE_THETA = 10000.0

VMEM_LIMIT_BYTES = 56 * 1024 * 1024
MAX_TOKEN_TILE = 1024
SCAN_ROWS = 2 * CHUNK
CONV_ROWS = 256
CONV_COLS = 2048
CONV_HALO = 16
ATTN_LAT_Q_ROWS = 512
ATTN_CHAIN_ROWS = 512
ATTN_KEY_BLOCK = 256
SCAN_GROUPS_PER_STEP = 8
FINAL_ROWS = 256
PROLOGUE_ROWS = 256
QKV_EPILOGUE_ROWS = 128
LOG2E = math.log2(math.e)


def _params(n_axes):
    return pltpu.CompilerParams(dimension_semantics=("arbitrary",) * n_axes,
                                vmem_limit_bytes=VMEM_LIMIT_BYTES)


def _silu(v):
    return v * jax.nn.sigmoid(v)


def _softplus(v):
    return jnp.maximum(v, 0.0) + jnp.log1p(jnp.exp(-jnp.abs(v)))


def _dot(a, b):
    return jnp.dot(a, b, preferred_element_type=F32)


def _split2(v):
    hi = v.astype(BF16)
    lo = (v - hi.astype(F32)).astype(BF16)
    return hi, lo


def _split3(v):
    hi = v.astype(BF16)
    r = v - hi.astype(F32)
    mid = r.astype(BF16)
    lo = (r - mid.astype(F32)).astype(BF16)
    return hi, mid, lo


def _adaln_kernel(c_ref, w_ref, b_ref, o_ref):
    s = _silu(c_ref[...]).astype(BF16)
    o_ref[0] = _dot(s, w_ref[0].astype(BF16)) + b_ref[0]


def _adaln(cond, w_mod, b_mod):
    depth, d, n = w_mod.shape
    rows = cond.shape[0]
    tn = 1536
    assert n % tn == 0
    return pl.pallas_call(
        _adaln_kernel,
        grid=(depth, n // tn),
        in_specs=[pl.BlockSpec((rows, d), lambda l, j: (0, 0)),
                  pl.BlockSpec((1, d, tn), lambda l, j: (l, 0, j)),
                  pl.BlockSpec((1, 1, tn), lambda l, j: (l, 0, j))],
        out_specs=pl.BlockSpec((1, rows, tn), lambda l, j: (l, 0, j)),
        out_shape=jax.ShapeDtypeStruct((depth, rows, n), F32),
        compiler_params=_params(2),
        name="adaln",
    )(cond, w_mod, b_mod.reshape(depth, 1, n))


class _Geom:
    def __init__(self, n_p, len_p, n_s, len_s):
        self.n_p, self.len_p, self.n_s, self.len_s = n_p, len_p, n_s, len_s
        self.t_p = n_p * len_p
        self.t_s = n_s * len_s
        self.t = self.t_p + self.t_s
        self.tm = math.gcd(math.gcd(self.t_p, len_s), MAX_TOKEN_TILE)

    def mod_row(self, i, tile):
        npt = self.t_p // tile
        per_seq = self.len_s // tile
        return jnp.where(i < npt, 0, 1 + (i - npt) // per_seq)


def _col_blocks(n_cols, block):
    return [slice(c, c + block) for c in range(0, n_cols, block)]


def _row_blocks(n_rows, block):
    return [slice(r0, r0 + block) for r0 in range(0, n_rows, block)]


def _norm_mod_rows(x_ref, mod_ref, g_ref, rows, shift_row):
    x = x_ref[rows, :]
    y = x * lax.rsqrt(jnp.mean(x * x, axis=-1, keepdims=True) + EPS) * g_ref[...]
    scale1 = 1.0 + mod_ref[0, shift_row + 1:shift_row + 2, :]
    return (y * scale1 + mod_ref[0, shift_row:shift_row + 1, :]).astype(BF16)


def _ssd_in_kernel(mod_ref, g_ref, w_ref, wdt_ref, *rest, z_blocks, sub_rows, n_p_tiles):
    x_refs = rest[:-4]
    z_ref, xbc_ref, dt_ref, h_ref = rest[-4:]
    j = pl.program_id(1)
    first = j == 0

    def first_step(x_ref):
        for rows in _row_blocks(z_ref.shape[0], sub_rows):
            h = _norm_mod_rows(x_ref, mod_ref, g_ref, rows, 0)
            h_ref[rows, :] = h
            z_ref[rows, :] = _dot(h, w_ref[...]).astype(z_ref.dtype)

    if len(x_refs) == 1:
        pl.when(first)(lambda: first_step(x_refs[0]))
    else:
        is_p = pl.program_id(0) < n_p_tiles
        pl.when(jnp.logical_and(first, is_p))(lambda: first_step(x_refs[0]))
        pl.when(jnp.logical_and(first, jnp.logical_not(is_p)))(lambda: first_step(x_refs[1]))

    @pl.when(jnp.logical_not(first))
    def _():
        h = h_ref[...]
        acc = _dot(h, w_ref[...])

        @pl.when(j < z_blocks)
        def _():
            z_ref[...] = acc.astype(z_ref.dtype)

        @pl.when(j >= z_blocks)
        def _():
            xbc_ref[...] = acc.astype(xbc_ref.dtype)

        @pl.when(j == pl.num_programs(1) - 1)
        def _():
            dt_ref[...] = _dot(h, wdt_ref[...])


def _ssd_in_proj(geom, x, mods, gain, w, d_inner, conv_dim, n_dt, tn):
    t, d = geom.t, w.shape[0]
    tm = geom.tm
    assert d_inner % tn == 0 and conv_dim % tn == 0 and (d_inner + conv_dim) % n_dt == 0
    z_blocks, xbc_blocks = d_inner // tn, conv_dim // tn
    dt_block = (d_inner + conv_dim) // n_dt
    xs, x_specs = _residual_specs(geom, x, (tm, d), lambda j: 0, single_buffer=True)
    return pl.pallas_call(
        functools.partial(_ssd_in_kernel, z_blocks=z_blocks, sub_rows=min(tm, PROLOGUE_ROWS),
                          n_p_tiles=geom.t_p // tm),
        grid=(t // tm, z_blocks + xbc_blocks),
        in_specs=[pl.BlockSpec((1, 6, d), lambda i, j: (geom.mod_row(i, tm), 0, 0)),
                  pl.BlockSpec((1, d), lambda i, j: (0, 0)),
                  pl.BlockSpec((d, tn), lambda i, j: (0, j)),
                  pl.BlockSpec((d, n_dt), lambda i, j: (0, dt_block))] + x_specs,
        out_specs=[pl.BlockSpec((tm, tn), lambda i, j: (i, jnp.minimum(j, z_blocks - 1))),
                   pl.BlockSpec((tm, tn), lambda i, j: (i, jnp.maximum(j - z_blocks, 0))),
                   pl.BlockSpec((tm, n_dt), lambda i, j: (i, 0))],
        out_shape=[jax.ShapeDtypeStruct((t, d_inner), BF16),
                   jax.ShapeDtypeStruct((t, conv_dim), BF16),
                   jax.ShapeDtypeStruct((t, n_dt), F32)],
        scratch_shapes=[pltpu.VMEM((tm, d), BF16)],
        compiler_params=_params(2),
        name="ssd_in",
    )(mods, gain.reshape(1, d), w, w, *xs)


def _weight_spec(w, layer, k, tn, col_block0):
    if w.ndim == 2:
        return pl.BlockSpec((k, tn), lambda i, j: (0, col_block0 + j))
    return pl.BlockSpec((None, k, tn), lambda i, j: (layer, 0, col_block0 + j))


def _nm_kernel(x_ref, mod_ref, g_ref, *rest, n_w, shift_row, act, sub_rows):
    w_refs = rest[:n_w]
    o_ref = rest[n_w]
    h_ref = rest[n_w + 1]

    def emit(rows, h):
        if act == "swiglu":
            gate = _dot(h, w_refs[0][...])
            up = _dot(h, w_refs[1][...])
            o_ref[rows, :] = (_silu(gate) * up).astype(o_ref.dtype)
        else:
            o_ref[rows, :] = _dot(h, w_refs[0][...]).astype(o_ref.dtype)

    first = pl.program_id(1) == 0

    @pl.when(first)
    def _():
        for rows in _row_blocks(o_ref.shape[0], sub_rows):
            h = _norm_mod_rows(x_ref, mod_ref, g_ref, rows, shift_row)
            h_ref[rows, :] = h
            emit(rows, h)

    @pl.when(jnp.logical_not(first))
    def _():
        emit(slice(None), h_ref[...])


def _norm_mod_matmul(geom, x, mods, gain, w, layer, col_offsets, n_cols, tn, shift_row, act, out_dtype, name):
    t, d = x.shape
    tm = geom.tm
    assert n_cols % tn == 0 and all(off % tn == 0 for off in col_offsets)
    w_specs = [_weight_spec(w, layer, d, tn, off // tn) for off in col_offsets]
    return pl.pallas_call(
        functools.partial(_nm_kernel, n_w=len(col_offsets), shift_row=shift_row, act=act,
                          sub_rows=min(tm, PROLOGUE_ROWS)),
        grid=(t // tm, n_cols // tn),
        in_specs=[pl.BlockSpec((tm, d), lambda i, j: (i, 0)),
                  pl.BlockSpec((1, 6, d), lambda i, j: (geom.mod_row(i, tm), 0, 0)),
                  pl.BlockSpec((1, d), lambda i, j: (0, 0))] + w_specs,
        out_specs=pl.BlockSpec((tm, tn), lambda i, j: (i, j)),
        out_shape=jax.ShapeDtypeStruct((t, n_cols), out_dtype),
        scratch_shapes=[pltpu.VMEM((tm, d), BF16)],
        compiler_params=_params(2),
        name=name,
    )(x, mods, gain.reshape(1, d), *([w] * len(col_offsets)))


def _gate_res_kernel(a_ref, w_ref, mod_ref, *rest, gate_row, n_p_tiles):
    x_refs, o_ref = rest[:-1], rest[-1]
    gated = mod_ref[0, gate_row:gate_row + 1, :] * _dot(a_ref[...], w_ref[...])
    if len(x_refs) == 1:
        o_ref[...] = x_refs[0][...] + gated
    else:
        is_p = pl.program_id(0) < n_p_tiles

        @pl.when(is_p)
        def _():
            o_ref[...] = x_refs[0][...] + gated

        @pl.when(jnp.logical_not(is_p))
        def _():
            o_ref[...] = x_refs[1][...] + gated


def _residual_specs(geom, x, block, col_of, single_buffer=False):
    tm = geom.tm
    npt = geom.t_p // tm
    if not isinstance(x, tuple):
        return [x], [pl.BlockSpec(block, lambda i, j: (i, col_of(j)))]
    mode = dict(pipeline_mode=pl.Buffered(1)) if single_buffer else {}
    return list(x), [pl.BlockSpec(block, lambda i, j: (jnp.minimum(i, npt - 1), col_of(j)), **mode),
                     pl.BlockSpec(block, lambda i, j: (jnp.maximum(i - npt, 0), col_of(j)))]


def _matmul_gate_res(geom, a, w, layer, x, mods, gate_row, tn, name):
    t, k = a.shape
    d = w.shape[-1]
    tm = geom.tm
    assert d % tn == 0
    xs, x_specs = _residual_specs(geom, x, (tm, tn), lambda j: j)
    return pl.pallas_call(
        functools.partial(_gate_res_kernel, gate_row=gate_row, n_p_tiles=geom.t_p // tm),
        grid=(t // tm, d // tn),
        in_specs=[pl.BlockSpec((tm, k), lambda i, j: (i, 0)),
                  _weight_spec(w, layer, k, tn, 0),
                  pl.BlockSpec((1, 6, tn), lambda i, j: (geom.mod_row(i, tm), 0, j))] + x_specs,
        out_specs=pl.BlockSpec((tm, tn), lambda i, j: (i, j)),
        out_shape=jax.ShapeDtypeStruct((t, d), F32),
        compiler_params=_params(2),
        name=name,
    )(a, w, mods, *xs)


def _gate_res_norm_kernel(a_ref, w_ref, x_ref, mod_ref, g_ref, yp_ref, ys_ref, *, gate_row, n_p_tiles):
    x = x_ref[...] + mod_ref[0, gate_row:gate_row + 1, :] * _dot(a_ref[...], w_ref[...])
    y = x * lax.rsqrt(jnp.mean(x * x, axis=-1, keepdims=True) + EPS) * g_ref[...]
    is_p = pl.program_id(0) < n_p_tiles

    @pl.when(is_p)
    def _():
        yp_ref[...] = y

    @pl.when(jnp.logical_not(is_p))
    def _():
        ys_ref[...] = y


def _matmul_gate_res_norm(geom, a, w, layer, x, mods, gate_row, gain):
    t, k = a.shape
    d = w.shape[-1]
    tm = math.gcd(math.gcd(geom.t_p, geom.len_s), FINAL_ROWS)
    npt = geom.t_p // tm
    w_block = (k, d) if w.ndim == 2 else (None, k, d)
    w_index = (lambda i: (0, 0)) if w.ndim == 2 else (lambda i: (layer, 0, 0))
    return pl.pallas_call(
        functools.partial(_gate_res_norm_kernel, gate_row=gate_row, n_p_tiles=npt),
        grid=(t // tm,),
        in_specs=[pl.BlockSpec((tm, k), lambda i: (i, 0)),
                  pl.BlockSpec(w_block, w_index, pipeline_mode=pl.Buffered(1)),
                  pl.BlockSpec((tm, d), lambda i: (i, 0)),
                  pl.BlockSpec((1, 6, d), lambda i: (geom.mod_row(i, tm), 0, 0)),
                  pl.BlockSpec((1, d), lambda i: (0, 0))],
        out_specs=[pl.BlockSpec((tm, d), lambda i: (jnp.minimum(i, npt - 1), 0)),
                   pl.BlockSpec((tm, d), lambda i: (jnp.maximum(i - npt, 0), 0))],
        out_shape=[jax.ShapeDtypeStruct((geom.t_p, d), F32),
                   jax.ShapeDtypeStruct((geom.t_s, d), F32)],
        compiler_params=_params(1),
        name="ffn_down_final",
    )(a, w, x, mods, gain.reshape(1, d))


def _conv_kernel(u_ref, prev_ref, next_ref, w_ref, b_ref, o_ref, ext_ref, *, n_p_blocks, p_blocks, s_blocks):
    i = pl.program_id(0)
    lb = u_ref.shape[0]
    is_p = i < n_p_blocks
    pos = jnp.where(is_p, i % p_blocks, (i - n_p_blocks) % s_blocks)
    last = jnp.where(is_p, p_blocks, s_blocks) - 1
    ext_ref[0:CONV_HALO, :] = jnp.where(pos > 0, prev_ref[...].astype(F32), 0.0)
    ext_ref[CONV_HALO:CONV_HALO + lb, :] = u_ref[...].astype(F32)
    ext_ref[CONV_HALO + lb:2 * CONV_HALO + lb, :] = jnp.where(pos < last, next_ref[...].astype(F32), 0.0)
    ext = ext_ref[...]
    n_ext = ext.shape[0]
    centre = D_CONV // 2
    acc = b_ref[...] + w_ref[centre:centre + 1, :] * ext[CONV_HALO:CONV_HALO + lb, :]
    for k in range(D_CONV):
        if k != centre:
            shifted = pltpu.roll(ext, (centre - k) % n_ext, 0)
            acc = acc + w_ref[k:k + 1, :] * shifted[CONV_HALO:CONV_HALO + lb, :]
    o_ref[...] = _silu(acc).astype(o_ref.dtype)


def _ssd_conv(geom, xbc, conv_w, conv_b):
    t, c = xbc.shape
    lb, ct = CONV_ROWS, 2048
    assert geom.len_p % lb == 0 and geom.len_s % lb == 0 and c % ct == 0
    hb = lb // CONV_HALO
    n_halo_blocks = t // CONV_HALO
    kern = functools.partial(_conv_kernel, n_p_blocks=geom.t_p // lb, p_blocks=geom.len_p // lb,
                             s_blocks=geom.len_s // lb)
    return pl.pallas_call(
        kern,
        grid=(t // lb, c // ct),
        in_specs=[pl.BlockSpec((lb, ct), lambda i, j: (i, j)),
                  pl.BlockSpec((CONV_HALO, ct), lambda i, j: (jnp.maximum(i * hb - 1, 0), j)),
                  pl.BlockSpec((CONV_HALO, ct), lambda i, j: (jnp.minimum((i + 1) * hb, n_halo_blocks - 1), j)),
                  pl.BlockSpec((D_CONV, ct), lambda i, j: (0, j)),
                  pl.BlockSpec((1, ct), lambda i, j: (0, j))],
        out_specs=pl.BlockSpec((lb, ct), lambda i, j: (i, j)),
        out_shape=jax.ShapeDtypeStruct((t, c), BF16),
        scratch_shapes=[pltpu.VMEM((lb + 2 * CONV_HALO, ct), F32)],
        compiler_params=_params(2),
        name="ssd_conv",
    )(xbc, xbc, xbc, conv_w, conv_b.reshape(1, c))


def _dt_prep_kernel(dt_ref, alog_ref, bias_ref, cs_ref, src_ref, e_ref, w_ref):
    q = CHUNK
    n_dir_heads = dt_ref.shape[1]
    li = lax.broadcasted_iota(jnp.int32, (q, q), 0)
    si = lax.broadcasted_iota(jnp.int32, (q, q), 1)
    lower = (li >= si).astype(BF16)
    upper = (li <= si).astype(BF16)
    fwd = lax.broadcasted_iota(jnp.int32, (1, n_dir_heads), 1) < n_dir_heads // 2
    a_scale = -jnp.exp(alog_ref[...]) * LOG2E
    for r0 in range(0, dt_ref.shape[0], q):
        dt = _softplus(dt_ref[r0:r0 + q, :] + bias_ref[...])
        parts = _split3(dt * a_scale)
        cs = jnp.where(fwd, sum(_dot(lower, p) for p in parts), sum(_dot(upper, p) for p in parts))
        total = jnp.where(fwd, cs[q - 1:q, :], cs[0:1, :])
        cs_ref[:, r0:r0 + q] = cs.T
        src_ref[:, r0:r0 + q] = (cs - jnp.log2(dt)).T
        e_ref[:, r0:r0 + q] = jnp.exp2(cs).T
        w_ref[:, r0:r0 + q] = (dt * jnp.exp2(total - cs)).T


def _dt_prep(dt_raw, a_log, dt_bias):
    t, nh = dt_raw.shape
    rb = 1024 if t % 1024 == 0 else SCAN_ROWS
    out = jax.ShapeDtypeStruct((nh, t), F32)
    head_major = pl.BlockSpec((nh, rb), lambda i: (0, i))
    return pl.pallas_call(
        _dt_prep_kernel,
        grid=(t // rb,),
        in_specs=[pl.BlockSpec((rb, nh), lambda i: (i, 0)),
                  pl.BlockSpec((1, nh), lambda i: (0, 0)),
                  pl.BlockSpec((1, nh), lambda i: (0, 0))],
        out_specs=[head_major] * 4,
        out_shape=[out] * 4,
        compiler_params=_params(1),
        name="ssd_dt_prep",
    )(dt_raw, a_log.reshape(1, nh), dt_bias.reshape(1, nh))


def _scan_kernel(*refs, rev, fuse_norm, fuse_conv, n_steps, n_p_steps, p_steps, s_steps, hpg, gps):
    if fuse_norm:
        (x_ref, b_ref, c_ref, cs_ref, src_ref, e_ref, w_ref, s0_ref, yo_ref, z_ref, dsk_ref, ng_ref,
         y_ref, sfin_ref, st_ref) = refs
    elif fuse_conv:
        (u_ref, prev_ref, next_ref, cw_ref, cb_ref, cs_ref, src_ref, e_ref, w_ref, s0_ref,
         y_ref, sfin_ref, xc_ref, st_ref, ext_ref) = refs
        x_ref = b_ref = c_ref = xc_ref
    else:
        (x_ref, b_ref, c_ref, cs_ref, src_ref, e_ref, w_ref, s0_ref,
         y_ref, sfin_ref, st_ref) = refs
    q = CHUNK
    hd = SSD_HEAD_DIM
    n = D_STATE
    width = hpg * hd
    step = pl.program_id(1)
    blk = (n_steps - 1 - step) if rev else step
    is_p = blk < n_p_steps
    pos = jnp.where(is_p, blk % p_steps, (blk - n_p_steps) % s_steps)
    last = jnp.where(is_p, p_steps, s_steps) - 1
    seq_start = (pos == last) if rev else (pos == 0)
    seq_end = (pos == 0) if rev else (pos == last)

    @pl.when(jnp.logical_and(seq_start, is_p))
    def _():
        st_ref[...] = jnp.zeros_like(st_ref)

    @pl.when(jnp.logical_and(seq_start, jnp.logical_not(is_p)))
    def _():
        for gi in range(gps):
            st_ref[gi] = s0_ref[gi].T

    li = lax.broadcasted_iota(jnp.int32, (q, q), 0)
    si = lax.broadcasted_iota(jnp.int32, (q, q), 1)
    keep = (li <= si) if rev else (li >= si)
    expand = (lax.broadcasted_iota(jnp.int32, (2 * hpg, width), 1) // hd
              == lax.broadcasted_iota(jnp.int32, (2 * hpg, width), 0) % hpg).astype(BF16)
    lane = lax.broadcasted_iota(jnp.int32, (q, 2 * hd), 1)
    edge = 0 if rev else q - 1

    def widen(v_r):
        hi, lo = _split2(v_r)
        return lax.dot_general(jnp.concatenate([hi, lo], axis=0), expand, (((0,), (0,)), ((), ())),
                               preferred_element_type=F32)

    b0 = gps * width if fuse_conv else 0
    c0 = b0 + gps * n if fuse_conv else 0
    if fuse_conv:
        lb = u_ref.shape[0]
        centre = D_CONV // 2
        for cblk in _col_blocks(u_ref.shape[1], ext_ref.shape[1]):
            ext_ref[0:CONV_HALO, :] = jnp.where(pos > 0, prev_ref[:, cblk].astype(F32), 0.0)
            ext_ref[CONV_HALO:CONV_HALO + lb, :] = u_ref[:, cblk].astype(F32)
            ext_ref[CONV_HALO + lb:2 * CONV_HALO + lb, :] = jnp.where(pos < last, next_ref[:, cblk].astype(F32), 0.0)
            ext = ext_ref[...]
            acc = cb_ref[:, cblk] + cw_ref[centre:centre + 1, cblk] * ext[CONV_HALO:CONV_HALO + lb, :]
            for k in range(D_CONV):
                if k != centre:
                    shifted = pltpu.roll(ext, (centre - k) % ext.shape[0], 0)
                    acc = acc + cw_ref[k:k + 1, cblk] * shifted[CONV_HALO:CONV_HALO + lb, :]
            xc_ref[:, cblk] = _silu(acc).astype(xc_ref.dtype)

    def chunk(gi, r0):
        cols = slice(gi * width, (gi + 1) * width)
        heads = slice(gi * hpg, (gi + 1) * hpg)
        x = x_ref[r0:r0 + q, cols]
        bm = b_ref[r0:r0 + q, b0 + gi * n:b0 + (gi + 1) * n]
        cm = c_ref[r0:r0 + q, c0 + gi * n:c0 + (gi + 1) * n]
        cs_c = cs_ref[heads, r0:r0 + q].T
        src_r = src_ref[heads, r0:r0 + q]
        e_x = widen(e_ref[heads, r0:r0 + q])
        w_x = widen(w_ref[heads, r0:r0 + q])

        cb = lax.dot_general(cm, bm, (((1,), (1,)), ((), ())), preferred_element_type=F32)
        state = st_ref[gi]
        y = _dot(cm, state.astype(BF16)) * e_x
        parts = []
        for p in range(hpg // 2):
            ms = []
            for r in (2 * p, 2 * p + 1):
                seg = cs_c[:, r:r + 1] - src_r[r:r + 1, :]
                ms.append((cb * jnp.exp2(jnp.where(keep, seg, -jnp.inf))).astype(BF16))
            x2 = x[:, 2 * p * hd:2 * (p + 1) * hd]
            zero = jnp.zeros_like(x2)
            rhs = jnp.concatenate([jnp.where(lane < hd, x2, zero), jnp.where(lane >= hd, x2, zero)], axis=0)
            parts.append(_dot(jnp.concatenate(ms, axis=1), rhs))
        y = y + jnp.concatenate(parts, axis=1)

        xf = x.astype(F32)
        xw = (xf * w_x).astype(BF16)
        st_ref[gi] = state * e_x[edge:edge + 1, :] + lax.dot_general(
            bm, xw, (((0,), (0,)), ((), ())), preferred_element_type=F32)

        if fuse_norm:
            y = y + yo_ref[r0:r0 + q, cols].astype(F32) + dsk_ref[gi] * xf
            yz = y * _silu(z_ref[r0:r0 + q, cols].astype(F32))
            yz = yz * lax.rsqrt(jnp.mean(yz * yz, axis=-1, keepdims=True) + EPS)
            y = yz * ng_ref[gi]
        y_ref[r0:r0 + q, cols] = y.astype(y_ref.dtype)

    n_chunks = y_ref.shape[0] // q
    for ci in (range(n_chunks - 1, -1, -1) if rev else range(n_chunks)):
        for gi in range(gps):
            chunk(gi, ci * q)

    @pl.when(jnp.logical_and(seq_end, is_p))
    def _():
        for gi in range(gps):
            sfin_ref[gi] = st_ref[gi].T


def _ssd_scan(geom, direction, xbc, decays, s0, y_other=None, z=None, dsk=None, ng=None, conv=None):
    t = xbc.shape[0]
    g_n = SSD_GROUPS
    gps = SCAN_GROUPS_PER_STEP
    hpg = decays[0].shape[0] // (2 * g_n)
    width = hpg * SSD_HEAD_DIM
    n = D_STATE
    rb = SCAN_ROWS
    assert geom.len_p % rb == 0 and geom.len_s % rb == 0 and g_n % gps == 0
    n_steps = t // rb
    n_p_steps, p_steps, s_steps = geom.t_p // rb, geom.len_p // rb, geom.len_s // rb
    rev = direction == 1
    fuse = y_other is not None
    x_cols = g_n * width
    b_blk0 = x_cols // (gps * n)
    c_blk0 = (x_cols + g_n * n) // (gps * n)
    head_blk0 = direction * (g_n // gps)

    def blk(s):
        return (n_steps - 1 - s) if rev else s

    def s0_idx(g, s):
        return (jnp.clip((blk(s) - n_p_steps) // s_steps, 0, geom.n_s - 1), g, 0, 0)

    def sfin_idx(g, s):
        return (jnp.minimum(blk(s) // p_steps, geom.n_p - 1), g, 0, 0)

    wide = pl.BlockSpec((rb, gps * width), lambda g, s: (blk(s), g))
    head_major = pl.BlockSpec((gps * hpg, rb), lambda g, s: (head_blk0 + g, blk(s)))
    in_specs = [wide,
                pl.BlockSpec((rb, gps * n), lambda g, s: (blk(s), b_blk0 + g)),
                pl.BlockSpec((rb, gps * n), lambda g, s: (blk(s), c_blk0 + g)),
                head_major, head_major, head_major, head_major,
                pl.BlockSpec((None, gps, width, n), s0_idx)]
    args = [xbc, xbc, xbc, *decays, s0]
    out_specs = [wide, pl.BlockSpec((None, gps, width, n), sfin_idx)]
    out_shape = [jax.ShapeDtypeStruct((t, x_cols), BF16),
                 jax.ShapeDtypeStruct((geom.n_p, g_n, width, n), F32)]
    scratch = [pltpu.VMEM((gps, n, width), F32)]
    if conv is not None:
        assert gps == g_n and rb == CONV_ROWS
        c_all = xbc.shape[1]
        hb = rb // CONV_HALO
        n_halo = t // CONV_HALO
        in_specs = [pl.BlockSpec((rb, c_all), lambda g, s: (blk(s), 0)),
                    pl.BlockSpec((CONV_HALO, c_all), lambda g, s: (jnp.maximum(blk(s) * hb - 1, 0), 0)),
                    pl.BlockSpec((CONV_HALO, c_all), lambda g, s: (jnp.minimum((blk(s) + 1) * hb, n_halo - 1), 0)),
                    pl.BlockSpec((D_CONV, c_all), lambda g, s: (0, 0)),
                    pl.BlockSpec((1, c_all), lambda g, s: (0, 0))] + in_specs[3:]
        args = [xbc, xbc, xbc, conv[0], conv[1].reshape(1, c_all)] + args[3:]
        out_specs.append(pl.BlockSpec((rb, c_all), lambda g, s: (blk(s), 0)))
        out_shape.append(jax.ShapeDtypeStruct((t, c_all), BF16))
        scratch.append(pltpu.VMEM((rb + 2 * CONV_HALO, CONV_COLS), F32))
    if fuse:
        in_specs += [wide, wide,
                     pl.BlockSpec((gps, 1, width), lambda g, s: (g, 0, 0)),
                     pl.BlockSpec((gps, 1, width), lambda g, s: (g, 0, 0))]
        args += [y_other, z, dsk, ng]
    kern = functools.partial(_scan_kernel, rev=rev, fuse_norm=fuse, fuse_conv=conv is not None, n_steps=n_steps,
                             n_p_steps=n_p_steps, p_steps=p_steps, s_steps=s_steps, hpg=hpg, gps=gps)
    return pl.pallas_call(
        kern,
        grid=(g_n // gps, n_steps),
        in_specs=in_specs,
        out_specs=out_specs,
        out_shape=out_shape,
        scratch_shapes=scratch,
        compiler_params=_params(2),
        name="ssd_scan_bwd" if rev else "ssd_scan_fwd",
    )(*args)


def _attn_qkv_kernel(x_ref, mod_ref, g_ref, w_ref, cos_ref, sin_ref, qg_ref, kg_ref,
                     q_ref, kc_ref, vc_ref, kl_ref, vl_ref, kf_ref, vf_ref, h_ref,
                     *, n_p_tiles, q_blocks, sub_rows):
    i = pl.program_id(0)
    j = pl.program_id(1)
    hd = ATTN_HEAD_DIM
    is_p = i < n_p_tiles
    is_s = jnp.logical_not(is_p)
    first = j == 0
    tm, tn = q_ref.shape
    heads = tn // hd
    blocks = _row_blocks(tm, sub_rows)
    lane = lax.broadcasted_iota(jnp.int32, (sub_rows, hd), 1)
    first_half = (lane % (hd // 2)) < (hd // 4)

    def norm(v, gain):
        return v * lax.rsqrt(jnp.mean(v * v, axis=-1, keepdims=True) + EPS) * gain

    def rope(v, c, s):
        partner = jnp.where(first_half, pltpu.roll(v, hd - hd // 4, 1), pltpu.roll(v, hd // 4, 1))
        return v * c + partner * s

    def q_rows(rows, h):
        q_scale = hd ** -0.5 * LOG2E
        acc = _dot(h, w_ref[...])
        cos_q = jnp.where(is_p, 1.0, cos_ref[rows, :]) * q_scale
        sin_q = jnp.where(is_p, 0.0, sin_ref[rows, :]) * q_scale
        for hh in range(heads):
            cols = slice(hh * hd, (hh + 1) * hd)
            q_ref[rows, cols] = rope(norm(acc[:, cols], qg_ref[...]), cos_q, sin_q).astype(q_ref.dtype)

    @pl.when(first)
    def _():
        for rows in blocks:
            h = _norm_mod_rows(x_ref, mod_ref, g_ref, rows, 0)
            h_ref[rows, :] = h
            q_rows(rows, h)

    @pl.when(jnp.logical_and(jnp.logical_not(first), j < q_blocks))
    def _():
        for rows in blocks:
            q_rows(rows, h_ref[rows, :])

    @pl.when(jnp.logical_and(j == q_blocks, is_p))
    def _():
        for rows in blocks:
            acc = _dot(h_ref[rows, :], w_ref[...])
            for hh in range(heads):
                cols = slice(hh * hd, (hh + 1) * hd)
                normed = norm(acc[:, cols], kg_ref[...])
                kf_ref[rows, cols] = normed
                kc_ref[rows, cols] = normed.astype(kc_ref.dtype)

    @pl.when(jnp.logical_and(j == q_blocks, is_s))
    def _():
        for rows in blocks:
            acc = _dot(h_ref[rows, :], w_ref[...])
            for hh in range(heads):
                cols = slice(hh * hd, (hh + 1) * hd)
                normed = norm(acc[:, cols], kg_ref[...])
                kl_ref[rows, cols] = rope(normed, cos_ref[rows, :], sin_ref[rows, :]).astype(kl_ref.dtype)

    @pl.when(jnp.logical_and(j == q_blocks + 1, is_p))
    def _():
        acc = _dot(h_ref[...], w_ref[...])
        vf_ref[...] = acc
        vc_ref[...] = acc.astype(vc_ref.dtype)

    @pl.when(jnp.logical_and(j == q_blocks + 1, is_s))
    def _():
        vl_ref[...] = _dot(h_ref[...], w_ref[...]).astype(vl_ref.dtype)


def _attn_qkv(geom, x, mods, gain, w, cos, sin, q_gain, k_gain, n_heads, past):
    t, d = x.shape
    tm = geom.tm
    hd, n_kv = ATTN_HEAD_DIM, N_KV_HEADS
    tn = n_kv * hd
    q_cols = n_heads * hd
    assert q_cols % tn == 0 and q_cols >= tn and w.shape[1] == q_cols + 2 * tn
    q_blocks = q_cols // tn
    npt = geom.t_p // tm
    per_seq = geom.len_s // tm

    def tab_idx(i, j):
        return (jnp.where(i < npt, 0, (i - npt) % per_seq), 0)

    def ctx_idx(i, j):
        return (jnp.minimum(i, npt - 1), 0)

    def lat_idx(i, j):
        return (jnp.clip((i - npt) // per_seq, 0, geom.n_s - 1), jnp.where(i < npt, 0, (i - npt) % per_seq), 0)

    kv_lat = jax.ShapeDtypeStruct((geom.n_s, geom.len_s + past, tn), BF16)
    return pl.pallas_call(
        functools.partial(_attn_qkv_kernel, n_p_tiles=npt, q_blocks=q_blocks,
                          sub_rows=min(tm, QKV_EPILOGUE_ROWS)),
        grid=(t // tm, q_blocks + 2),
        in_specs=[pl.BlockSpec((tm, d), lambda i, j: (i, 0)),
                  pl.BlockSpec((1, 6, d), lambda i, j: (geom.mod_row(i, tm), 0, 0)),
                  pl.BlockSpec((1, d), lambda i, j: (0, 0)),
                  pl.BlockSpec((d, tn), lambda i, j: (0, j)),
                  pl.BlockSpec((tm, hd), tab_idx),
                  pl.BlockSpec((tm, hd), tab_idx),
                  pl.BlockSpec((1, hd), lambda i, j: (0, 0)),
                  pl.BlockSpec((1, hd), lambda i, j: (0, 0))],
        out_specs=[pl.BlockSpec((tm, tn), lambda i, j: (i, jnp.minimum(j, q_blocks - 1))),
                   pl.BlockSpec((tm, tn), ctx_idx),
                   pl.BlockSpec((tm, tn), ctx_idx),
                   pl.BlockSpec((None, tm, tn), lat_idx),
                   pl.BlockSpec((None, tm, tn), lat_idx),
                   pl.BlockSpec((tm, tn), ctx_idx),
                   pl.BlockSpec((tm, tn), ctx_idx)],
        out_shape=[jax.ShapeDtypeStruct((t, q_cols), BF16),
                   jax.ShapeDtypeStruct((geom.t_p, tn), BF16),
                   jax.ShapeDtypeStruct((geom.t_p, tn), BF16),
                   kv_lat, kv_lat,
                   jax.ShapeDtypeStruct((geom.t_p, tn), F32),
                   jax.ShapeDtypeStruct((geom.t_p, tn), F32)],
        scratch_shapes=[pltpu.VMEM((tm, d), BF16)],
        compiler_params=_params(2),
        name="attn_qkv",
    )(x, mods, gain.reshape(1, d), w, cos, sin, q_gain.reshape(1, hd), k_gain.reshape(1, hd))


def _append_cache_kernel(ck_ref, cv_ref, k_in, v_in, k_ref, v_ref):
    k_ref[...] = ck_ref[...].astype(k_ref.dtype)
    v_ref[...] = cv_ref[...].astype(v_ref.dtype)


def _append_cache(k_lat, v_lat, cache_k, cache_v, len_s):
    n_s, past, kv = cache_k.shape
    assert len_s % past == 0
    cache_spec = pl.BlockSpec((None, past, kv), lambda b: (b, 0, 0))
    tail_spec = pl.BlockSpec((None, past, kv), lambda b: (b, len_s // past, 0))
    any_spec = pl.BlockSpec(memory_space=pl.ANY)
    return pl.pallas_call(
        _append_cache_kernel,
        grid=(n_s,),
        in_specs=[cache_spec, cache_spec, any_spec, any_spec],
        out_specs=[tail_spec, tail_spec],
        out_shape=[jax.ShapeDtypeStruct(k_lat.shape, k_lat.dtype)] * 2,
        input_output_aliases={2: 0, 3: 1},
        compiler_params=_params(1),
        name="append_cache",
    )(cache_k, cache_v, k_lat, v_lat)


def _attn_ctx_kernel(q_ref, k_ref, v_ref, o_ref, *, rep):
    hd = ATTN_HEAD_DIM
    rows = q_ref.shape[0]
    for kv in range(k_ref.shape[1] // hd):
        k = k_ref[:, kv * hd:(kv + 1) * hd]
        v = v_ref[:, kv * hd:(kv + 1) * hd]
        heads = range(kv * rep, (kv + 1) * rep)
        q = jnp.concatenate([q_ref[:, r * hd:(r + 1) * hd] for r in heads], axis=0)
        s = lax.dot_general(q, k, (((1,), (1,)), ((), ())), preferred_element_type=F32)
        p = jnp.exp2(s - jnp.max(s, axis=-1, keepdims=True))
        denom = jnp.sum(p, axis=-1, keepdims=True)
        o = _dot(p.astype(BF16), v) / denom
        for n, r in enumerate(heads):
            o_ref[:, r * hd:(r + 1) * hd] = o[n * rows:(n + 1) * rows, :].astype(o_ref.dtype)


def _attn_stream_kernel(q_ref, k_ref, v_ref, o_in, o_ref, *, rep, chain_rows, key_block):
    del o_in
    hd = ATTN_HEAD_DIM
    n_keys = k_ref.shape[0]
    for r in range(rep):
        cols = slice(r * hd, (r + 1) * hd)
        for rows in _row_blocks(q_ref.shape[0], chain_rows):
            q = q_ref[rows, cols]
            m = jnp.full((chain_rows, 1), -jnp.inf, F32)
            denom = jnp.zeros((chain_rows, 1), F32)
            acc = jnp.zeros((chain_rows, hd), F32)
            for c0 in range(0, n_keys, key_block):
                keys = slice(c0, c0 + key_block)
                s = lax.dot_general(q, k_ref[keys, :], (((1,), (1,)), ((), ())), preferred_element_type=F32)
                m_new = jnp.maximum(m, jnp.max(s, axis=-1, keepdims=True))
                alpha = jnp.exp2(m - m_new)
                p = jnp.exp2(s - m_new)
                denom = alpha * denom + jnp.sum(p, axis=-1, keepdims=True)
                acc = alpha * acc + _dot(p.astype(BF16), v_ref[keys, :])
                m = m_new
            o_ref[rows, cols] = (acc / denom).astype(o_ref.dtype)


def _attention(q, k_ctx, v_ctx, k_lat, v_lat, geom, rep):
    t, qd = q.shape
    hd, n_kv = ATTN_HEAD_DIM, N_KV_HEADS
    out_shape = jax.ShapeDtypeStruct((t, qd), BF16)
    o = pl.pallas_call(
        functools.partial(_attn_ctx_kernel, rep=rep),
        grid=(geom.n_p,),
        in_specs=[pl.BlockSpec((geom.len_p, qd), lambda b: (b, 0)),
                  pl.BlockSpec((geom.len_p, n_kv * hd), lambda b: (b, 0)),
                  pl.BlockSpec((geom.len_p, n_kv * hd), lambda b: (b, 0))],
        out_specs=pl.BlockSpec((geom.len_p, qd), lambda b: (b, 0)),
        out_shape=out_shape,
        compiler_params=_params(1),
        name="attn_ctx",
    )(q, k_ctx, v_ctx)
    qb = math.gcd(math.gcd(geom.len_s, geom.t_p), ATTN_LAT_Q_ROWS)
    blk_q = (qb, rep * hd)
    s_blocks = geom.len_s // qb
    q0 = geom.t_p // qb
    l_all = k_lat.shape[1]
    return pl.pallas_call(
        functools.partial(_attn_stream_kernel, rep=rep, chain_rows=math.gcd(qb, ATTN_CHAIN_ROWS),
                          key_block=math.gcd(l_all, ATTN_KEY_BLOCK)),
        grid=(geom.n_s, n_kv, s_blocks),
        in_specs=[pl.BlockSpec(blk_q, lambda b, h, i: (q0 + b * s_blocks + i, h)),
                  pl.BlockSpec((None, l_all, hd), lambda b, h, i: (b, 0, h)),
                  pl.BlockSpec((None, l_all, hd), lambda b, h, i: (b, 0, h)),
                  pl.BlockSpec(memory_space=pl.ANY)],
        out_specs=pl.BlockSpec(blk_q, lambda b, h, i: (q0 + b * s_blocks + i, h)),
        out_shape=out_shape,
        input_output_aliases={3: 0},
        compiler_params=_params(3),
        name="attn_lat",
    )(q, k_lat, v_lat, o)


def _rope_tables(length):
    half = ATTN_HEAD_DIM // 2
    pos = jnp.arange(length, dtype=jnp.int32)
    row_pos = (pos // GRID_W).astype(F32)
    col_pos = (pos % GRID_W).astype(F32)
    inv = ROPE_THETA ** (-jnp.arange(0, half, 2, dtype=F32) / half)
    ang_r = row_pos[:, None] * inv[None, :]
    ang_c = col_pos[:, None] * inv[None, :]
    cos = jnp.concatenate([jnp.cos(ang_r)] * 2 + [jnp.cos(ang_c)] * 2, axis=-1)
    sin = jnp.concatenate([-jnp.sin(ang_r), jnp.sin(ang_r), -jnp.sin(ang_c), jnp.sin(ang_c)], axis=-1)
    return cos, sin


def kernel(x_prompt, x_sample, state_ssd_fwd, state_ssd_bwd, cache_k, cache_v, c, c_ctx, w_mod, b_mod, norm_mix, norm_ffn, ssd_w_in, ssd_conv_w, ssd_conv_b, ssd_a_log, ssd_dt_bias, ssd_d, ssd_norm, ssd_w_out, attn_w_qkv, attn_q_norm, attn_k_norm, attn_w_o, ffn_w_gu, ffn_w_down, final_norm):
    n_p, len_p, d = x_prompt.shape
    n_s, len_s, _ = x_sample.shape
    depth = w_mod.shape[0]
    geom = _Geom(n_p, len_p, n_s, len_s)
    d_ff = ffn_w_down.shape[1]
    d_inner = ssd_w_out.shape[1]
    heads = d_inner // SSD_HEAD_DIM
    hpg = heads // SSD_GROUPS
    gw = hpg * SSD_HEAD_DIM
    conv_dim = d_inner + 2 * SSD_GROUPS * D_STATE
    n_heads = attn_w_o.shape[1] // ATTN_HEAD_DIM
    rep = n_heads // N_KV_HEADS

    x = (x_prompt.reshape(geom.t_p, d), x_sample.reshape(geom.t_s, d))
    w_gu_all = ffn_w_gu.astype(BF16)
    w_down_all = ffn_w_down.astype(BF16)

    cond_rows = -(-(1 + n_s) // 8) * 8
    cond = jnp.zeros((cond_rows, d), F32).at[0].set(c_ctx).at[1:1 + n_s].set(c)
    mods_all = _adaln(cond, w_mod, b_mod).reshape(depth, cond_rows, 6, d)

    new_f, new_b, new_k, new_v = [], [], [], []
    for i in range(depth):
        mods = mods_all[i]
        j = i // 2
        if i % 2 == 0:
            w_in = ssd_w_in[j].astype(BF16)
            z, xbc, dt_raw = _ssd_in_proj(geom, x, mods, norm_mix[i], w_in, d_inner, conv_dim, 2 * heads, 1024)
            decays = _dt_prep(dt_raw, ssd_a_log[j], ssd_dt_bias[j])
            dsk = jnp.repeat(ssd_d[j, 0] + ssd_d[j, 1], SSD_HEAD_DIM).reshape(SSD_GROUPS, 1, gw)
            ng = ssd_norm[j].reshape(SSD_GROUPS, 1, gw)
            s0_f = state_ssd_fwd[:, j].reshape(n_s, SSD_GROUPS, gw, D_STATE)
            s0_b = state_ssd_bwd[:, j].reshape(n_s, SSD_GROUPS, gw, D_STATE)
            y_b, s_b, xbc = _ssd_scan(geom, 1, xbc, decays, s0_b, conv=(ssd_conv_w[j], ssd_conv_b[j]))
            y, s_f = _ssd_scan(geom, 0, xbc, decays, s0_f, y_b, z, dsk, ng)
            new_f.append(s_f.reshape(n_p, heads, SSD_HEAD_DIM, D_STATE))
            new_b.append(s_b.reshape(n_p, heads, SSD_HEAD_DIM, D_STATE))
            x = _matmul_gate_res(geom, y, ssd_w_out[j].astype(BF16), 0, x, mods, 2, 512, "ssd_out")
        else:
            w_qkv = attn_w_qkv[j].astype(BF16)
            cos, sin = _rope_tables(len_s)
            kv_w = N_KV_HEADS * ATTN_HEAD_DIM
            past = cache_k.shape[2]
            q, k_c, v_c, k_lat, v_lat, k_ctx, v_ctx = _attn_qkv(
                geom, x, mods, norm_mix[i], w_qkv, cos, sin, attn_q_norm[j], attn_k_norm[j], n_heads, past)
            k_lat, v_lat = _append_cache(k_lat, v_lat, cache_k[:, j].reshape(n_s, past, kv_w),
                                         cache_v[:, j].reshape(n_s, past, kv_w), len_s)
            o = _attention(q, k_c, v_c, k_lat, v_lat, geom, rep)
            new_k.append(k_ctx.reshape(n_p, len_p, N_KV_HEADS, ATTN_HEAD_DIM))
            new_v.append(v_ctx.reshape(n_p, len_p, N_KV_HEADS, ATTN_HEAD_DIM))
            x = _matmul_gate_res(geom, o, attn_w_o[j].astype(BF16), 0, x, mods, 2, 1024, "attn_out")
        hidden = _norm_mod_matmul(geom, x, mods, norm_ffn[i], w_gu_all, i, [0, d_ff], d_ff, 512, 3, "swiglu", BF16,
                                  "ffn_up")
        if i < depth - 1:
            x = _matmul_gate_res(geom, hidden, w_down_all, i, x, mods, 5, 512, "ffn_down")
        else:
            y_prompt, y_sample = _matmul_gate_res_norm(geom, hidden, w_down_all, i, x, mods, 5, final_norm)

    y_prompt = y_prompt.reshape(n_p, len_p, d)
    y_sample = y_sample.reshape(n_s, len_s, d)
    return (y_prompt, y_sample, jnp.stack(new_f, axis=1), jnp.stack(new_b, axis=1),
            jnp.stack(new_k, axis=1), jnp.stack(new_v, axis=1))
```
